```python
import jax, jax.numpy as jnp
from jax import lax
import numpy as np

D_MODEL = 2048
BATCH = 16
SEQ = 2048
DEPTH = 2

GRID_W = 64
CTX_LEN = 256
MLA_HEADS = 8
QK_NOPE = 128
QK_ROPE = 64
V_HEAD = 128
Q_LORA = 512
KV_LORA = 512
N_FREQ = QK_ROPE // 4
ROPE_THETA = 10000.0
MLA_SCALE = (QK_NOPE + QK_ROPE) ** -0.5
Q_BLOCK = 128
NA_HEADS = 8
NA_HEAD_DIM = 128
WIN_H = 8
WIN_W = 16
NA_SCALE = NA_HEAD_DIM ** -0.5
D_MLA_OUT = MLA_HEADS * V_HEAD
D_NA = NA_HEADS * NA_HEAD_DIM
D_MIX = D_MLA_OUT + D_NA
P_IN = Q_LORA + KV_LORA + QK_ROPE + 3 * D_NA
D_FF = 5632
N_MOD = 9
EPS = 1e-6
NEG = -1e30

kernel_name = "hymba_mla_natten_macaron_dit"


def rms_norm(x, g):
    xf = x.astype(jnp.float32)
    y = xf * lax.rsqrt(jnp.mean(xf * xf, axis=-1, keepdims=True) + EPS)
    return (y * g.astype(jnp.float32)).astype(x.dtype)


def modulate(x, shift, scale):
    return x * (1 + scale) + shift


def swiglu(h, w1, w3, w2):
    return (jax.nn.silu(h @ w1) * (h @ w3)) @ w2


def axial_rope_tables(n_tokens):
    t = jnp.arange(n_tokens, dtype=jnp.int32)
    pos = jnp.stack([t // GRID_W, t % GRID_W], axis=-1).astype(jnp.float32)
    inv_freq = ROPE_THETA ** (-jnp.arange(N_FREQ, dtype=jnp.float32) / N_FREQ)
    ang = pos[:, :, None] * inv_freq
    return jnp.cos(ang), jnp.sin(ang)


def apply_axial_rope(x, cos, sin):
    lead = x.shape[:-1]
    xr = x.astype(jnp.float32).reshape(lead + (2, 2, N_FREQ))
    a, b = xr[..., 0, :], xr[..., 1, :]
    expand = (cos.shape[0],) + (1,) * (x.ndim - 3) + cos.shape[1:]
    cs, sn = cos.reshape(expand), sin.reshape(expand)
    out = jnp.stack([a * cs - b * sn, a * sn + b * cs], axis=-2)
    return out.reshape(x.shape).astype(x.dtype)


def mla_qkv(cq, ckv, qa_norm, w_uq, kva_norm, w_ukv):
    B, T = cq.shape[:2]
    q = (rms_norm(cq, qa_norm) @ w_uq).reshape(B, T, MLA_HEADS, QK_NOPE + QK_ROPE)
    kv = (rms_norm(ckv, kva_norm) @ w_ukv).reshape(B, T, MLA_HEADS, QK_NOPE + V_HEAD)
    return q[..., :QK_NOPE], q[..., QK_NOPE:], kv[..., :QK_NOPE], kv[..., QK_NOPE:]


def mla_attend(qn, qr, kn, kr, v):
    s = jnp.einsum('bqhd,bkhd->bhqk', qn, kn) + jnp.einsum('bqhr,bkr->bhqk', qr, kr)
    p = jax.nn.softmax(s.astype(jnp.float32) * MLA_SCALE, axis=-1).astype(v.dtype)
    return jnp.einsum('bhqk,bkhd->bqhd', p, v)


def mla_latent(qn, qr, kn, kr, v):
    B, S = qn.shape[:2]
    nb = S // Q_BLOCK

    def to_blocks(t):
        return jnp.moveaxis(t.reshape((B, nb, Q_BLOCK) + t.shape[2:]), 1, 0)

    o = lax.map(lambda qb: mla_attend(qb[0], qb[1], kn, kr, v), (to_blocks(qn), to_blocks(qr)))
    return jnp.moveaxis(o, 0, 1).reshape(B, S, D_MLA_OUT)


def dense_attention(q, k, v, scale):
    s = jnp.einsum('bqhd,bkhd->bhqk', q, k).astype(jnp.float32) * scale
    p = jax.nn.softmax(s, axis=-1).astype(v.dtype)
    return jnp.einsum('bhqk,bkhd->bqhd', p, v)


def natten_latent(q, k, v, k_c, v_c, rpb):
    B, S, H, Dh = q.shape
    rows = S // GRID_W
    wh = min(WIN_H, rows)
    qg = q.reshape(B, rows, GRID_W, H, Dh)
    kg = k.reshape(B, rows, GRID_W, H, Dh)
    vg = v.reshape(B, rows, GRID_W, H, Dh)
    cols = jnp.arange(GRID_W, dtype=jnp.int32)
    col_start = jnp.clip(cols - WIN_W // 2, 0, GRID_W - WIN_W)
    col_ok = (cols[None, :] >= col_start[:, None]) & (cols[None, :] < col_start[:, None] + WIN_W)
    mask = jnp.broadcast_to(col_ok[:, None, :], (GRID_W, wh, GRID_W)).reshape(GRID_W, wh * GRID_W)
    dc_idx = jnp.clip(cols[None, :] - cols[:, None] + WIN_W - 1, 0, 2 * WIN_W - 2)
    n_lat = wh * GRID_W

    def one_row(r):
        rs = jnp.clip(r - wh // 2, 0, rows - wh)
        q_r = lax.dynamic_index_in_dim(qg, r, axis=1, keepdims=False)
        k_s = lax.dynamic_slice_in_dim(kg, rs, wh, axis=1).reshape(B, n_lat, H, Dh)
        v_s = lax.dynamic_slice_in_dim(vg, rs, wh, axis=1).reshape(B, n_lat, H, Dh)
        dr_idx = rs + jnp.arange(wh, dtype=jnp.int32) - r + WIN_H - 1
        bias = rpb[:, dr_idx[:, None, None], dc_idx[None, :, :]]
        bias = bias.transpose(0, 2, 1, 3).reshape(H, GRID_W, n_lat).astype(jnp.float32)
        s_lat = jnp.einsum('bqhd,bkhd->bhqk', q_r, k_s).astype(jnp.float32) * NA_SCALE + bias
        s_lat = jnp.where(mask, s_lat, NEG)
        s_ctx = jnp.einsum('bqhd,bkhd->bhqk', q_r, k_c).astype(jnp.float32) * NA_SCALE
        p = jax.nn.softmax(jnp.concatenate([s_lat, s_ctx], axis=-1), axis=-1).astype(v.dtype)
        return (jnp.einsum('bhqk,bkhd->bqhd', p[..., :n_lat], v_s)
                + jnp.einsum('bhqk,bkhd->bqhd', p[..., n_lat:], v_c))

    o = lax.map(one_row, jnp.arange(rows, dtype=jnp.int32))
    return jnp.moveaxis(o, 0, 1).reshape(B, S, D_NA)


def split_in(p):
    i0 = Q_LORA
    i1 = i0 + KV_LORA
    i2 = i1 + QK_ROPE
    i3 = i2 + D_NA
    i4 = i3 + D_NA
    return jnp.split(p, [i0, i1, i2, i3, i4], axis=-1)


def merge_groups(a, b, out_norm, w_out):
    y = jnp.concatenate([rms_norm(a, out_norm[:D_MLA_OUT]), rms_norm(b, out_norm[D_MLA_OUT:])], axis=-1)
    return y @ w_out


def hybrid_mixer(h_lat, h_ctx, w_in, qa_norm, w_uq, kva_norm, w_ukv, rpb, out_norm, w_out, ctx_out):
    B, S, _ = h_lat.shape
    heads = lambda t: t.reshape(t.shape[0], t.shape[1], NA_HEADS, NA_HEAD_DIM)
    cq, ckv, kr, nq, nk, nv = split_in(h_lat @ w_in)
    cq_c, ckv_c, kr_c, nq_c, nk_c, nv_c = split_in(h_ctx @ w_in)
    qn, qr, kn, v = mla_qkv(cq, ckv, qa_norm, w_uq, kva_norm, w_ukv)
    qn_c, qr_c, kn_c, v_c = mla_qkv(cq_c, ckv_c, qa_norm, w_uq, kva_norm, w_ukv)
    cos, sin = axial_rope_tables(S)
    qr = apply_axial_rope(qr, cos, sin)
    kr = apply_axial_rope(kr, cos, sin)
    a_lat = mla_latent(qn, qr,
                       jnp.concatenate([kn, kn_c], axis=1),
                       jnp.concatenate([kr, kr_c], axis=1),
                       jnp.concatenate([v, v_c], axis=1))
    nk_c, nv_c = heads(nk_c), heads(nv_c)
    b_lat = natten_latent(heads(nq), heads(nk), heads(nv), nk_c, nv_c, rpb)
    y_lat = merge_groups(a_lat, b_lat, out_norm, w_out)
    if not ctx_out:
        return y_lat, None
    Bc, T = h_ctx.shape[:2]
    a_ctx = mla_attend(qn_c, qr_c, kn_c, kr_c, v_c).reshape(Bc, T, D_MLA_OUT)
    b_ctx = dense_attention(heads(nq_c), nk_c, nv_c, NA_SCALE).reshape(Bc, T, D_NA)
    return y_lat, merge_groups(a_ctx, b_ctx, out_norm, w_out)


def setup_inputs(seed: int = 0) -> dict:
    key = jax.random.key(seed)
    ks = iter(jax.random.split(key, 32))
    nrm = lambda shape, std: std * jax.random.normal(next(ks), shape, jnp.float32)
    gain = lambda shape: 1.0 + 0.01 * jax.random.normal(next(ks), shape, jnp.float32)
    L = DEPTH
    return {
        "x": nrm((BATCH, SEQ, D_MODEL), 1.0),
        "c": nrm((BATCH, D_MODEL), 1.0),
        "ctx": nrm((BATCH, CTX_LEN, D_MODEL), 1.0),
        "c_ctx": nrm((D_MODEL,), 1.0),
        "ada_w": nrm((L, D_MODEL, N_MOD * D_MODEL), 0.5 * D_MODEL ** -0.5),
        "ada_b": nrm((L, N_MOD * D_MODEL), 0.01),
        "ffn1_norm": gain((L, D_MODEL)),
        "ffn1_w1": nrm((L, D_MODEL, D_FF), D_MODEL ** -0.5),
        "ffn1_w3": nrm((L, D_MODEL, D_FF), D_MODEL ** -0.5),
        "ffn1_w2": nrm((L, D_FF, D_MODEL), D_FF ** -0.5),
        "mix_norm": gain((L, D_MODEL)),
        "w_in": nrm((L, D_MODEL, P_IN), D_MODEL ** -0.5),
        "qa_norm": gain((L, Q_LORA)),
        "w_uq": nrm((L, Q_LORA, MLA_HEADS * (QK_NOPE + QK_ROPE)), Q_LORA ** -0.5),
        "kva_norm": gain((L, KV_LORA)),
        "w_ukv": nrm((L, KV_LORA, MLA_HEADS * (QK_NOPE + V_HEAD)), KV_LORA ** -0.5),
        "na_rpb": nrm((L, NA_HEADS, 2 * WIN_H - 1, 2 * WIN_W - 1), 0.1),
        "out_norm": gain((L, D_MIX)),
        "w_out": nrm((L, D_MIX, D_MODEL), D_MIX ** -0.5),
        "ffn2_norm": gain((L, D_MODEL)),
        "ffn2_w1": nrm((L, D_MODEL, D_FF), D_MODEL ** -0.5),
        "ffn2_w3": nrm((L, D_MODEL, D_FF), D_MODEL ** -0.5),
        "ffn2_w2": nrm((L, D_FF, D_MODEL), D_FF ** -0.5),
        "final_norm": gain((D_MODEL,)),
    }


def reference(x, c, ctx, c_ctx, ada_w, ada_b, ffn1_norm, ffn1_w1, ffn1_w3, ffn1_w2, mix_norm, w_in,
              qa_norm, w_uq, kva_norm, w_ukv, na_rpb, out_norm, w_out, ffn2_norm, ffn2_w1, ffn2_w3,
              ffn2_w2, final_norm):
    s_lat = jax.nn.silu(c)[:, None, :]
    s_ctx = jax.nn.silu(c_ctx)[None, None, :]
    h, hc = x, ctx
    for l in range(DEPTH):
        last = l == DEPTH - 1
        m = jnp.split(s_lat @ ada_w[l] + ada_b[l], N_MOD, axis=-1)
        mc = jnp.split(s_ctx @ ada_w[l] + ada_b[l], N_MOD, axis=-1)
        h = h + 0.5 * m[2] * swiglu(modulate(rms_norm(h, ffn1_norm[l]), m[0], m[1]),
                                    ffn1_w1[l], ffn1_w3[l], ffn1_w2[l])
        hc = hc + 0.5 * mc[2] * swiglu(modulate(rms_norm(hc, ffn1_norm[l]), mc[0], mc[1]),
                                       ffn1_w1[l], ffn1_w3[l], ffn1_w2[l])
        y, yc = hybrid_mixer(modulate(rms_norm(h, mix_norm[l]), m[3], m[4]),
                             modulate(rms_norm(hc, mix_norm[l]), mc[3], mc[4]),
                             w_in[l], qa_norm[l], w_uq[l], kva_norm[l], w_ukv[l], na_rpb[l],
                             out_norm[l], w_out[l], not last)
        h = h + m[5] * y
        h = h + 0.5 * m[8] * swiglu(modulate(rms_norm(h, ffn2_norm[l]), m[6], m[7]),
                                    ffn2_w1[l], ffn2_w3[l], ffn2_w2[l])
        if not last:
            hc = hc + mc[5] * yc
            hc = hc + 0.5 * mc[8] * swiglu(modulate(rms_norm(hc, ffn2_norm[l]), mc[6], mc[7]),
                                           ffn2_w1[l], ffn2_w3[l], ffn2_w2[l])
    return rms_norm(h, final_norm)
```

```python
import functools

import numpy as np
import jax
import jax.numpy as jnp
from jax import lax
from jax.experimental import pallas as pl
from jax.experimental.pallas import tpu as pltpu

D_MODEL = 2048
BATCH = 16
SEQ = 2048
DEPTH = 2
GRID_W = 64
GRID_H = SEQ // GRID_W
CTX_LEN = 256
MLA_HEADS = 8
QK_NOPE = 128
QK_ROPE = 64
V_HEAD = 128
Q_LORA = 512
KV_LORA = 512
N_FREQ = QK_ROPE // 4
ROPE_THETA = 10000.0
MLA_SCALE = (QK_NOPE + QK_ROPE) ** -0.5
NA_HEADS = 8
NA_HEAD_DIM = 128
WIN_H = 8
WIN_W = 16
NA_SCALE = NA_HEAD_DIM ** -0.5
D_MLA_OUT = MLA_HEADS * V_HEAD
D_NA = NA_HEADS * NA_HEAD_DIM
D_FF = 5632
N_MOD = 9
EPS = 1e-6
NEG = -1e30

F32 = jnp.float32
BF16 = jnp.bfloat16

LANES = 128
V7X_VMEM_LIMIT = 56 * 1024 * 1024

MOD_ROWS = 24
CTX_MOD_ROW = BATCH
QK_PAD = 256
P1_COLS = 2 * Q_LORA + 2 * LANES
NA_GROUP_ROWS = 4
NA_WIN_ROWS = 12
N_NA_GROUPS = GRID_H // NA_GROUP_ROWS

TM = 512
TF = 512
TQ = 1024


def _params(sem, vmem=V7X_VMEM_LIMIT):
    return pltpu.CompilerParams(dimension_semantics=sem, vmem_limit_bytes=vmem)


def _rms(x, g):
    return x * lax.rsqrt(jnp.mean(x * x, axis=-1, keepdims=True) + EPS) * g


def _silu(a):
    return a * (1.0 / (1.0 + jnp.exp(-a)))


def _dot(a, b):
    return jnp.dot(a, b, preferred_element_type=F32)


def _dot_nt(a, b):
    return lax.dot_general(a, b, (((1,), (1,)), ((), ())), preferred_element_type=F32)


def _mod_kernel(s_ref, w_ref, b_ref, o_ref):
    s = _silu(s_ref[...])
    o_ref[...] = _dot(s.astype(BF16), w_ref[...].astype(BF16)) + b_ref[...]


def _modulation(c_rows, ada_w, ada_b):
    tn = 1024
    n = N_MOD * D_MODEL
    out = pl.pallas_call(
        _mod_kernel,
        grid=(DEPTH, n // tn),
        in_specs=[
            pl.BlockSpec((MOD_ROWS, D_MODEL), lambda l, j: (0, 0)),
            pl.BlockSpec((None, D_MODEL, tn), lambda l, j: (l, 0, j)),
            pl.BlockSpec((None, 1, tn), lambda l, j: (l, 0, j)),
        ],
        out_specs=pl.BlockSpec((None, MOD_ROWS, tn), lambda l, j: (l, 0, j)),
        out_shape=jax.ShapeDtypeStruct((DEPTH, MOD_ROWS, n), F32),
        compiler_params=_params(("parallel", "parallel")),
        name="adaln_mod",
    )(c_rows, ada_w, ada_b.reshape(DEPTH, 1, n))
    return out.reshape(DEPTH * MOD_ROWS, N_MOD, D_MODEL)


def _mod_spec(row_of_tile):
    return pl.BlockSpec((1, N_MOD, D_MODEL), lambda i, *_: (row_of_tile(i), 0, 0))


def _ffn_kernel(x_ref, mod_ref, g_ref, w1_ref, w3_ref, w2_ref, gf_ref, o_ref, xn_ref, *, k0, final):
    j = pl.program_id(1)

    @pl.when(j == 0)
    def _():
        y = _rms(x_ref[...], g_ref[...])
        xn_ref[...] = (y * (1.0 + mod_ref[0, k0 + 1:k0 + 2, :]) + mod_ref[0, k0:k0 + 1, :]).astype(BF16)

    xn = xn_ref[...]
    a = _dot(xn, w1_ref[...])
    b = _dot(xn, w3_ref[...])
    gated = (_silu(a) * b).astype(BF16)
    nc = 512
    for n in range(D_MODEL // nc):
        part = _dot(gated, w2_ref[:, n * nc:(n + 1) * nc])

        @pl.when(j == 0)
        def _():
            o_ref[:, n * nc:(n + 1) * nc] = part

        @pl.when(j > 0)
        def _():
            o_ref[:, n * nc:(n + 1) * nc] += part

    @pl.when(j == pl.num_programs(1) - 1)
    def _():
        h = x_ref[...] + (0.5 * mod_ref[0, k0 + 2:k0 + 3, :]) * o_ref[...]
        o_ref[...] = _rms(h, gf_ref[...]) if final else h


def _ffn(x, mod, row_of_tile, g, w1, w3, w2, k0, final_g=None):
    t = x.shape[0]
    final = final_g is not None
    gf = final_g if final else g
    return pl.pallas_call(
        functools.partial(_ffn_kernel, k0=k0, final=final),
        grid=(t // TM, D_FF // TF),
        in_specs=[
            pl.BlockSpec((TM, D_MODEL), lambda i, j: (i, 0)),
            _mod_spec(row_of_tile),
            pl.BlockSpec((1, D_MODEL), lambda i, j: (0, 0)),
            pl.BlockSpec((D_MODEL, TF), lambda i, j: (0, j)),
            pl.BlockSpec((D_MODEL, TF), lambda i, j: (0, j)),
            pl.BlockSpec((TF, D_MODEL), lambda i, j: (j, 0)),
            pl.BlockSpec((1, D_MODEL), lambda i, j: (0, 0)),
        ],
        out_specs=pl.BlockSpec((TM, D_MODEL), lambda i, j: (i, 0)),
        out_shape=jax.ShapeDtypeStruct((t, D_MODEL), F32),
        scratch_shapes=[pltpu.VMEM((TM, D_MODEL), BF16)],
        compiler_params=_params(("parallel", "arbitrary")),
        name="ffn",
    )(x, mod, g.reshape(1, D_MODEL), w1, w3, w2, gf.reshape(1, D_MODEL))


def _inproj_kernel(x_ref, mod_ref, g_ref, w_ref, cs_ref, o_ref, xn_ref, *, k0):
    @pl.when(pl.program_id(1) == 0)
    def _():
        y = _rms(x_ref[...], g_ref[...])
        xn_ref[...] = (y * (1.0 + mod_ref[0, k0 + 1:k0 + 2, :]) + mod_ref[0, k0:k0 + 1, :]).astype(BF16)

    o_ref[...] = (_dot(xn_ref[...], w_ref[...]) * cs_ref[...]).astype(o_ref.dtype)


def _inproj(x, mod, row_of_tile, g, w, col_scale, tn, out_dtype, k0):
    t = x.shape[0]
    n = w.shape[1]
    return pl.pallas_call(
        functools.partial(_inproj_kernel, k0=k0),
        grid=(t // TM, n // tn),
        in_specs=[
            pl.BlockSpec((TM, D_MODEL), lambda i, j: (i, 0)),
            _mod_spec(row_of_tile),
            pl.BlockSpec((1, D_MODEL), lambda i, j: (0, 0)),
            pl.BlockSpec((D_MODEL, tn), lambda i, j: (0, j)),
            pl.BlockSpec((1, tn), lambda i, j: (0, j)),
        ],
        out_specs=pl.BlockSpec((TM, tn), lambda i, j: (i, j)),
        out_shape=jax.ShapeDtypeStruct((t, n), out_dtype),
        scratch_shapes=[pltpu.VMEM((TM, D_MODEL), BF16)],
        compiler_params=_params(("parallel", "arbitrary")),
        name="inproj",
    )(x, mod, g.reshape(1, D_MODEL), w, col_scale)


def _q_kernel(cq_ref, g_ref, wa_ref, wb_ref, cos_ref, sin_ref, o_ref):
    xn = _rms(cq_ref[...], g_ref[...]).astype(BF16)
    ya = _dot(xn, wa_ref[...])
    yb = _dot(xn, wb_ref[...])
    cos, sin = cos_ref[...], sin_ref[...]
    for h in range(MLA_HEADS):
        lo = h * QK_PAD
        o_ref[:, lo:lo + LANES] = (ya[:, lo:lo + LANES] * MLA_SCALE).astype(BF16)
        rope = ya[:, lo + LANES:lo + QK_PAD] * cos + yb[:, h * LANES:(h + 1) * LANES] * sin
        o_ref[:, lo + LANES:lo + QK_PAD] = (rope * MLA_SCALE).astype(BF16)


def _q_heads(p1, g, wa, wb, cos_t, sin_t, table_block):
    t = p1.shape[0]
    return pl.pallas_call(
        _q_kernel,
        grid=(t // TM,),
        in_specs=[
            pl.BlockSpec((TM, Q_LORA), lambda i: (i, 0)),
            pl.BlockSpec((1, Q_LORA), lambda i: (0, 0)),
            pl.BlockSpec((Q_LORA, MLA_HEADS * QK_PAD), lambda i: (0, 0)),
            pl.BlockSpec((Q_LORA, MLA_HEADS * LANES), lambda i: (0, 0)),
            pl.BlockSpec((TM, LANES), lambda i: (table_block(i), 0)),
            pl.BlockSpec((TM, LANES), lambda i: (table_block(i), 0)),
        ],
        out_specs=pl.BlockSpec((TM, MLA_HEADS * QK_PAD), lambda i: (i, 0)),
        out_shape=jax.ShapeDtypeStruct((t, MLA_HEADS * QK_PAD), BF16),
        compiler_params=_params(("parallel",)),
        name="mla_q",
    )(p1, g.reshape(1, Q_LORA), wa, wb, cos_t, sin_t)


def _kv_kernel(ckv_ref, kra_ref, krb_ref, g_ref, wk_ref, wv_ref, cos_ref, sin_ref, k_ref, v_ref):
    xn = _rms(ckv_ref[...], g_ref[...]).astype(BF16)
    yk = _dot(xn, wk_ref[...])
    v_ref[...] = _dot(xn, wv_ref[...]).astype(BF16)
    kr = (kra_ref[...] * cos_ref[...] + krb_ref[...] * sin_ref[...]).astype(BF16)
    for h in range(MLA_HEADS):
        lo = h * QK_PAD
        k_ref[:, lo:lo + LANES] = yk[:, h * LANES:(h + 1) * LANES].astype(BF16)
        k_ref[:, lo + LANES:lo + QK_PAD] = kr


def _kv_heads(p1, g, wk, wv, cos_t, sin_t, table_block):
    t = p1.shape[0]
    kr_col = 2 * Q_LORA // LANES
    return pl.pallas_call(
        _kv_kernel,
        grid=(t // TM,),
        in_specs=[
            pl.BlockSpec((TM, KV_LORA), lambda i: (i, 1)),
            pl.BlockSpec((TM, LANES), lambda i: (i, kr_col)),
            pl.BlockSpec((TM, LANES), lambda i: (i, kr_col + 1)),
            pl.BlockSpec((1, KV_LORA), lambda i: (0, 0)),
            pl.BlockSpec((KV_LORA, MLA_HEADS * QK_NOPE), lambda i: (0, 0)),
            pl.BlockSpec((KV_LORA, D_MLA_OUT), lambda i: (0, 0)),
            pl.BlockSpec((TM, LANES), lambda i: (table_block(i), 0)),
            pl.BlockSpec((TM, LANES), lambda i: (table_block(i), 0)),
        ],
        out_specs=[
            pl.BlockSpec((TM, MLA_HEADS * QK_PAD), lambda i: (i, 0)),
            pl.BlockSpec((TM, D_MLA_OUT), lambda i: (i, 0)),
        ],
        out_shape=[
            jax.ShapeDtypeStruct((t, MLA_HEADS * QK_PAD), BF16),
            jax.ShapeDtypeStruct((t, D_MLA_OUT), BF16),
        ],
        compiler_params=_params(("parallel",)),
        name="mla_kv",
    )(p1, p1, p1, g.reshape(1, KV_LORA), wk, wv, cos_t, sin_t)


def _softmax_pv(s_parts, v_parts):
    m = functools.reduce(jnp.maximum, [jnp.max(s, axis=-1, keepdims=True) for s in s_parts])
    p_parts = [jnp.exp(s - m) for s in s_parts]
    l = functools.reduce(jnp.add, [jnp.sum(p, axis=-1, keepdims=True) for p in p_parts])
    o = functools.reduce(jnp.add, [_dot(p.astype(BF16), v) for p, v in zip(p_parts, v_parts)])
    return o * (1.0 / l)


def _mla_kernel(q_ref, kl_ref, kc_ref, vl_ref, vc_ref, o_ref):
    q = q_ref[...]
    o_ref[...] = _softmax_pv([_dot_nt(q, kl_ref[...]), _dot_nt(q, kc_ref[...])], [vl_ref[...], vc_ref[...]])


def _mla_latent(q, k_lat, k_ctx, v_lat, v_ctx):
    nq = SEQ // TQ
    return pl.pallas_call(
        _mla_kernel,
        grid=(BATCH, MLA_HEADS, nq),
        in_specs=[
            pl.BlockSpec((TQ, QK_PAD), lambda b, h, i: (b * nq + i, h)),
            pl.BlockSpec((SEQ, QK_PAD), lambda b, h, i: (b, h)),
            pl.BlockSpec((CTX_LEN, QK_PAD), lambda b, h, i: (b, h)),
            pl.BlockSpec((SEQ, V_HEAD), lambda b, h, i: (b, h)),
            pl.BlockSpec((CTX_LEN, V_HEAD), lambda b, h, i: (b, h)),
        ],
        out_specs=pl.BlockSpec((TQ, V_HEAD), lambda b, h, i: (b * nq + i, h)),
        out_shape=jax.ShapeDtypeStruct((BATCH * SEQ, D_MLA_OUT), F32),
        compiler_params=_params(("parallel", "parallel", "arbitrary")),
        name="mla_latent",
    )(q, k_lat, k_ctx, v_lat, v_ctx)


def _ctx_attn_kernel(q_ref, k_ref, v_ref, o_ref):
    o_ref[...] = _softmax_pv([_dot_nt(q_ref[...], k_ref[...])], [v_ref[...]])


def _ctx_attention(q, k, v, width, q_off, k_off, v_off):
    return pl.pallas_call(
        _ctx_attn_kernel,
        grid=(BATCH, MLA_HEADS),
        in_specs=[
            pl.BlockSpec((CTX_LEN, width), lambda b, h: (b, q_off + h)),
            pl.BlockSpec((CTX_LEN, width), lambda b, h: (b, k_off + h)),
            pl.BlockSpec((CTX_LEN, V_HEAD), lambda b, h: (b, v_off + h)),
        ],
        out_specs=pl.BlockSpec((CTX_LEN, V_HEAD), lambda b, h: (b, h)),
        out_shape=jax.ShapeDtypeStruct((BATCH * CTX_LEN, MLA_HEADS * V_HEAD), F32),
        compiler_params=_params(("parallel", "parallel")),
        name="ctx_attention",
    )(q, k, v)


def _na_group_window(g):
    if g == 0:
        return 0, 0
    if g == 1:
        return 1, 0
    if g == N_NA_GROUPS - 1:
        return 3, GRID_H - NA_WIN_ROWS
    return 2, NA_GROUP_ROWS * g - WIN_H // 2


def _na_kernel(q_ref, k_ref, v_ref, kc_ref, vc_ref, tab_ref, o_ref):
    kc, vc = kc_ref[...], vc_ref[...]
    gq = NA_GROUP_ROWS * GRID_W
    gk = NA_WIN_ROWS * GRID_W
    for g in range(N_NA_GROUPS):
        var, ks = _na_group_window(g)
        q = q_ref[g * gq:(g + 1) * gq, :]
        kw = k_ref[ks * GRID_W:ks * GRID_W + gk, :]
        vw = v_ref[ks * GRID_W:ks * GRID_W + gk, :]
        s_lat = _dot_nt(q, kw) + tab_ref[var]
        o_ref[g * gq:(g + 1) * gq, :] = _softmax_pv([s_lat, _dot_nt(q, kc)], [vw, vc])


def _na_tables(rpb):
    qa = np.arange(NA_GROUP_ROWS).reshape(-1, 1, 1, 1)
    qc = np.arange(GRID_W).reshape(1, -1, 1, 1)
    ki = np.arange(NA_WIN_ROWS).reshape(1, 1, -1, 1)
    kc = np.arange(GRID_W).reshape(1, 1, 1, -1)
    col_start = np.clip(qc - WIN_W // 2, 0, GRID_W - WIN_W)
    col_ok = (kc >= col_start) & (kc < col_start + WIN_W)
    dc = np.clip(kc - qc + WIN_W - 1, 0, 2 * WIN_W - 2)
    tables = []
    for g in (0, 1, 2, N_NA_GROUPS - 1):
        _, ks = _na_group_window(g)
        r = NA_GROUP_ROWS * g + qa
        rs = np.clip(r - WIN_H // 2, 0, GRID_H - WIN_H)
        krow = ks + ki
        ok = (krow >= rs) & (krow < rs + WIN_H) & col_ok
        dr = np.clip(krow - r + WIN_H - 1, 0, 2 * WIN_H - 2)
        shape = (NA_GROUP_ROWS, GRID_W, NA_WIN_ROWS, GRID_W)
        dr_i = np.broadcast_to(dr, shape).reshape(NA_GROUP_ROWS * GRID_W, NA_WIN_ROWS * GRID_W)
        dc_i = np.broadcast_to(dc, shape).reshape(NA_GROUP_ROWS * GRID_W, NA_WIN_ROWS * GRID_W)
        ok_i = np.broadcast_to(ok, shape).reshape(NA_GROUP_ROWS * GRID_W, NA_WIN_ROWS * GRID_W)
        bias = rpb[:, dr_i, dc_i].astype(F32)
        tables.append(jnp.where(ok_i[None], bias, NEG))
    return jnp.stack(tables)


def _na_latent(p2_lat, p2_ctx, tables):
    nh = NA_HEADS
    gq = NA_GROUP_ROWS * GRID_W
    gk = NA_WIN_ROWS * GRID_W
    return pl.pallas_call(
        _na_kernel,
        grid=(nh, BATCH),
        in_specs=[
            pl.BlockSpec((SEQ, NA_HEAD_DIM), lambda h, b: (b, h)),
            pl.BlockSpec((SEQ, NA_HEAD_DIM), lambda h, b: (b, nh + h)),
            pl.BlockSpec((SEQ, NA_HEAD_DIM), lambda h, b: (b, 2 * nh + h)),
            pl.BlockSpec((CTX_LEN, NA_HEAD_DIM), lambda h, b: (b, nh + h)),
            pl.BlockSpec((CTX_LEN, NA_HEAD_DIM), lambda h, b: (b, 2 * nh + h)),
            pl.BlockSpec((4, None, gq, gk), lambda h, b: (0, h, 0, 0)),
        ],
        out_specs=pl.BlockSpec((SEQ, NA_HEAD_DIM), lambda h, b: (b, h)),
        out_shape=jax.ShapeDtypeStruct((BATCH * SEQ, D_NA), F32),
        compiler_params=_params(("parallel", "parallel")),
        name="na_latent",
    )(p2_lat, p2_lat, p2_lat, p2_ctx, p2_ctx, tables)


def _merge_kernel(a_ref, b_ref, h_ref, mod_ref, ga_ref, gb_ref, wa_ref, wb_ref, o_ref, *, k0):
    an = _rms(a_ref[...], ga_ref[...]).astype(BF16)
    bn = _rms(b_ref[...], gb_ref[...]).astype(BF16)
    y = _dot(an, wa_ref[...]) + _dot(bn, wb_ref[...])
    o_ref[...] = h_ref[...] + mod_ref[0, k0:k0 + 1, :] * y


def _merge(a, b, h, mod, row_of_tile, out_norm, w_out, k0):
    t = h.shape[0]
    g2 = out_norm.reshape(2, 1, D_MLA_OUT)
    return pl.pallas_call(
        functools.partial(_merge_kernel, k0=k0),
        grid=(t // TM,),
        in_specs=[
            pl.BlockSpec((TM, D_MLA_OUT), lambda i: (i, 0)),
            pl.BlockSpec((TM, D_NA), lambda i: (i, 0)),
            pl.BlockSpec((TM, D_MODEL), lambda i: (i, 0)),
            _mod_spec(row_of_tile),
            pl.BlockSpec((None, 1, D_MLA_OUT), lambda i: (0, 0, 0)),
            pl.BlockSpec((None, 1, D_NA), lambda i: (1, 0, 0)),
            pl.BlockSpec((D_MLA_OUT, D_MODEL), lambda i: (0, 0)),
            pl.BlockSpec((D_NA, D_MODEL), lambda i: (1, 0)),
        ],
        out_specs=pl.BlockSpec((TM, D_MODEL), lambda i: (i, 0)),
        out_shape=jax.ShapeDtypeStruct((t, D_MODEL), F32),
        compiler_params=_params(("parallel",)),
        name="merge",
    )(a, b, h, mod, g2, g2, w_out, w_out)


def _rope_tables():
    t = np.arange(SEQ)
    pos = np.stack([t // GRID_W, t % GRID_W], axis=-1).astype(np.float32)
    inv_freq = jnp.asarray(ROPE_THETA, F32) ** (-jnp.arange(N_FREQ, dtype=F32) / N_FREQ)
    ang = jnp.asarray(pos)[:, :, None] * inv_freq
    cos, sin = jnp.cos(ang), jnp.sin(ang)
    cos_t = jnp.concatenate([cos, cos], axis=-1).reshape(SEQ, QK_ROPE)
    sin_t = jnp.concatenate([-sin, sin], axis=-1).reshape(SEQ, QK_ROPE)
    pad = jnp.zeros((SEQ, LANES - QK_ROPE), F32)
    cos_t = jnp.concatenate([cos_t, pad], axis=-1)
    sin_t = jnp.concatenate([sin_t, pad], axis=-1)
    ident = jnp.concatenate([jnp.ones((TM, QK_ROPE), F32), jnp.zeros((TM, LANES - QK_ROPE), F32)], axis=-1)
    return (jnp.concatenate([cos_t, ident], axis=0),
            jnp.concatenate([sin_t, jnp.zeros((TM, LANES), F32)], axis=0))


def _rotate_half_cols(w):
    blocks = w.reshape(w.shape[:-1] + (2, 2, N_FREQ))
    return blocks[..., ::-1, :].reshape(w.shape)


def _layer_weights(w_in, w_uq, w_ukv):
    kr = w_in[:, 2 * Q_LORA:2 * Q_LORA + QK_ROPE]
    zpad = jnp.zeros((D_MODEL, LANES - QK_ROPE), F32)
    w_p1 = jnp.concatenate([w_in[:, :2 * Q_LORA], kr, zpad, _rotate_half_cols(kr), zpad], axis=1).astype(BF16)
    w_p2 = w_in[:, 2 * Q_LORA + QK_ROPE:].astype(BF16)
    uq = w_uq.reshape(Q_LORA, MLA_HEADS, QK_NOPE + QK_ROPE)
    hz = jnp.zeros((Q_LORA, MLA_HEADS, QK_PAD - QK_NOPE - QK_ROPE), F32)
    wa = jnp.concatenate([uq, hz], axis=-1).reshape(Q_LORA, MLA_HEADS * QK_PAD).astype(BF16)
    wb = jnp.concatenate([_rotate_half_cols(uq[..., QK_NOPE:]), hz], axis=-1)
    wb = wb.reshape(Q_LORA, MLA_HEADS * LANES).astype(BF16)
    ukv = w_ukv.reshape(KV_LORA, MLA_HEADS, QK_NOPE + V_HEAD)
    wk = ukv[..., :QK_NOPE].reshape(KV_LORA, MLA_HEADS * QK_NOPE).astype(BF16)
    wv = ukv[..., QK_NOPE:].reshape(KV_LORA, D_MLA_OUT).astype(BF16)
    return w_p1, w_p2, wa, wb, wk, wv


def kernel(x, c, ctx, c_ctx, ada_w, ada_b, ffn1_norm, ffn1_w1, ffn1_w3, ffn1_w2, mix_norm, w_in, qa_norm, w_uq,
           kva_norm, w_ukv, na_rpb, out_norm, w_out, ffn2_norm, ffn2_w1, ffn2_w3, ffn2_w2, final_norm):
    c_rows = jnp.concatenate([c, c_ctx[None, :], jnp.zeros((MOD_ROWS - BATCH - 1, D_MODEL), F32)], axis=0)
    mod = _modulation(c_rows, ada_w, ada_b)
    cos_t, sin_t = _rope_tables()
    ones_p1 = jnp.ones((1, P1_COLS), F32)
    p2_scale = jnp.concatenate([jnp.full((1, D_NA), NA_SCALE, F32), jnp.ones((1, 2 * D_NA), F32)], axis=1)
    tiles_per_sample = SEQ // TM

    h = x.reshape(BATCH * SEQ, D_MODEL)
    hc = ctx.reshape(BATCH * CTX_LEN, D_MODEL)
    for l in range(DEPTH):
        last = l == DEPTH - 1
        lat_row = lambda i, l=l: l * MOD_ROWS + i // tiles_per_sample
        ctx_row = lambda i, l=l: l * MOD_ROWS + CTX_MOD_ROW
        lat_tab = lambda i: i % tiles_per_sample
        ctx_tab = lambda i: tiles_per_sample
        w1, w3, w2 = ffn1_w1[l].astype(BF16), ffn1_w3[l].astype(BF16), ffn1_w2[l].astype(BF16)
        h = _ffn(h, mod, lat_row, ffn1_norm[l], w1, w3, w2, 0)
        hc = _ffn(hc, mod, ctx_row, ffn1_norm[l], w1, w3, w2, 0)

        w_p1, w_p2, wa, wb, wk, wv = _layer_weights(w_in[l], w_uq[l], w_ukv[l])
        tables = _na_tables(na_rpb[l])
        streams = []
        for hs, row, tab in ((h, lat_row, lat_tab), (hc, ctx_row, ctx_tab)):
            p1 = _inproj(hs, mod, row, mix_norm[l], w_p1, ones_p1, P1_COLS, F32, 3)
            p2 = _inproj(hs, mod, row, mix_norm[l], w_p2, p2_scale, 1024, BF16, 3)
            k, v = _kv_heads(p1, kva_norm[l], wk, wv, cos_t, sin_t, tab)
            streams.append((p1, p2, k, v, row, tab))
        (p1, p2, k, v, _, _), (p1c, p2c, kc, vc, _, _) = streams

        q = _q_heads(p1, qa_norm[l], wa, wb, cos_t, sin_t, lat_tab)
        a_lat = _mla_latent(q, k, kc, v, vc)
        b_lat = _na_latent(p2, p2c, tables)
        h = _merge(a_lat, b_lat, h, mod, lat_row, out_norm[l], w_out[l].astype(BF16), 5)
        w1, w3, w2 = ffn2_w1[l].astype(BF16), ffn2_w3[l].astype(BF16), ffn2_w2[l].astype(BF16)
        h = _ffn(h, mod, lat_row, ffn2_norm[l], w1, w3, w2, 6, final_g=final_norm if last else None)
        if not last:
            qc = _q_heads(p1c, qa_norm[l], wa, wb, cos_t, sin_t, ctx_tab)
            a_ctx = _ctx_attention(qc, kc, vc, QK_PAD, 0, 0, 0)
            b_ctx = _ctx_attention(p2c, p2c, p2c, NA_HEAD_DIM, 0, NA_HEADS, 2 * NA_HEADS)
            hc = _merge(a_ctx, b_ctx, hc, mod, ctx_row, out_norm[l], w_out[l].astype(BF16), 5)
            hc = _ffn(hc, mod, ctx_row, ffn2_norm[l], w1, w3, w2, 6)
    return h.reshape(BATCH, SEQ, D_MODEL)
```

```python
import functools

import numpy as np
import jax
import jax.numpy as jnp
from jax import lax
from jax.experimental import pallas as pl
from jax.experimental.pallas import tpu as pltpu

D_MODEL = 2048
BATCH = 16
SEQ = 2048
DEPTH = 2
GRID_W = 64
GRID_H = SEQ // GRID_W
CTX_LEN = 256
MLA_HEADS = 8
QK_NOPE = 128
QK_ROPE = 64
V_HEAD = 128
Q_LORA = 512
KV_LORA = 512
N_FREQ = QK_ROPE // 4
ROPE_THETA = 10000.0
MLA_SCALE = (QK_NOPE + QK_ROPE) ** -0.5
NA_HEADS = 8
NA_HEAD_DIM = 128
WIN_H = 8
WIN_W = 16
NA_SCALE = NA_HEAD_DIM ** -0.5
D_MLA_OUT = MLA_HEADS * V_HEAD
D_NA = NA_HEADS * NA_HEAD_DIM
D_FF = 5632
N_MOD = 9
EPS = 1e-6
NEG = -1e30

F32 = jnp.float32
BF16 = jnp.bfloat16

LANES = 128
V7X_VMEM_LIMIT = 56 * 1024 * 1024

MOD_ROWS = 24
CTX_MOD_ROW = BATCH
QK_PAD = 256
P1_COLS = 2 * Q_LORA + 2 * LANES
NA_GROUP_ROWS = 4
NA_WIN_ROWS = 12
N_NA_GROUPS = GRID_H // NA_GROUP_ROWS

TM = 512
FFN_TM = 1024
TF = 512
TQ = 1024


def _params(sem, vmem=V7X_VMEM_LIMIT):
    return pltpu.CompilerParams(dimension_semantics=sem, vmem_limit_bytes=vmem)


def _rms(x, g):
    return x * lax.rsqrt(jnp.mean(x * x, axis=-1, keepdims=True) + EPS) * g


def _silu(a):
    return a * (1.0 / (1.0 + jnp.exp(-a)))


def _dot(a, b):
    return jnp.dot(a, b, preferred_element_type=F32)


def _dot_nt(a, b):
    return lax.dot_general(a, b, (((1,), (1,)), ((), ())), preferred_element_type=F32)


def _mod_kernel(s_ref, w_ref, b_ref, o_ref):
    s = _silu(s_ref[...])
    o_ref[...] = _dot(s.astype(BF16), w_ref[...].astype(BF16)) + b_ref[...]


def _modulation(c_rows, ada_w, ada_b):
    tn = 1024
    n = N_MOD * D_MODEL
    out = pl.pallas_call(
        _mod_kernel,
        grid=(DEPTH, n // tn),
        in_specs=[
            pl.BlockSpec((MOD_ROWS, D_MODEL), lambda l, j: (0, 0)),
            pl.BlockSpec((None, D_MODEL, tn), lambda l, j: (l, 0, j)),
            pl.BlockSpec((None, 1, tn), lambda l, j: (l, 0, j)),
        ],
        out_specs=pl.BlockSpec((None, MOD_ROWS, tn), lambda l, j: (l, 0, j)),
        out_shape=jax.ShapeDtypeStruct((DEPTH, MOD_ROWS, n), F32),
        compiler_params=_params(("parallel", "parallel")),
        name="adaln_mod",
    )(c_rows, ada_w, ada_b.reshape(DEPTH, 1, n))
    return out.reshape(DEPTH * MOD_ROWS, N_MOD, D_MODEL)


def _mod_spec(row_of_tile, tm):
    return pl.BlockSpec((1, N_MOD, D_MODEL), lambda i, *_: (row_of_tile(i, tm), 0, 0))


def _ffn_kernel(x_ref, mod_ref, g_ref, w1_ref, w3_ref, w2_ref, gf_ref, o_ref, xn_ref, *, k0, final):
    j = pl.program_id(1)

    @pl.when(j == 0)
    def _():
        y = _rms(x_ref[...], g_ref[...])
        xn_ref[...] = (y * (1.0 + mod_ref[0, k0 + 1:k0 + 2, :]) + mod_ref[0, k0:k0 + 1, :]).astype(BF16)

    xn = xn_ref[...]
    a = _dot(xn, w1_ref[...])
    b = _dot(xn, w3_ref[...])
    gated = (_silu(a) * b).astype(BF16)
    nc = 512
    for n in range(D_MODEL // nc):
        part = _dot(gated, w2_ref[:, n * nc:(n + 1) * nc])

        @pl.when(j == 0)
        def _():
            o_ref[:, n * nc:(n + 1) * nc] = part

        @pl.when(j > 0)
        def _():
            o_ref[:, n * nc:(n + 1) * nc] += part

    @pl.when(j == pl.num_programs(1) - 1)
    def _():
        h = x_ref[...] + (0.5 * mod_ref[0, k0 + 2:k0 + 3, :]) * o_ref[...]
        o_ref[...] = _rms(h, gf_ref[...]) if final else h


def _ffn(x, mod, row_of_tile, g, w1, w3, w2, k0, final_g=None):
    t = x.shape[0]
    final = final_g is not None
    gf = final_g if final else g
    return pl.pallas_call(
        functools.partial(_ffn_kernel, k0=k0, final=final),
        grid=(t // FFN_TM, D_FF // TF),
        in_specs=[
            pl.BlockSpec((FFN_TM, D_MODEL), lambda i, j: (i, 0), pipeline_mode=pl.Buffered(1)),
            _mod_spec(row_of_tile, FFN_TM),
            pl.BlockSpec((1, D_MODEL), lambda i, j: (0, 0)),
            pl.BlockSpec((D_MODEL, TF), lambda i, j: (0, j)),
            pl.BlockSpec((D_MODEL, TF), lambda i, j: (0, j)),
            pl.BlockSpec((TF, D_MODEL), lambda i, j: (j, 0)),
            pl.BlockSpec((1, D_MODEL), lambda i, j: (0, 0)),
        ],
        out_specs=pl.BlockSpec((FFN_TM, D_MODEL), lambda i, j: (i, 0)),
        out_shape=jax.ShapeDtypeStruct((t, D_MODEL), F32),
        scratch_shapes=[pltpu.VMEM((FFN_TM, D_MODEL), BF16)],
        compiler_params=_params(("parallel", "arbitrary")),
        name="ffn",
    )(x, mod, g.reshape(1, D_MODEL), w1, w3, w2, gf.reshape(1, D_MODEL))


def _inproj_kernel(x_ref, mod_ref, g_ref, w_ref, cs_ref, o_ref, xn_ref, *, k0):
    @pl.when(pl.program_id(1) == 0)
    def _():
        y = _rms(x_ref[...], g_ref[...])
        xn_ref[...] = (y * (1.0 + mod_ref[0, k0 + 1:k0 + 2, :]) + mod_ref[0, k0:k0 + 1, :]).astype(BF16)

    o_ref[...] = (_dot(xn_ref[...], w_ref[...]) * cs_ref[...]).astype(o_ref.dtype)


def _inproj(x, mod, row_of_tile, g, w, col_scale, tn, out_dtype, k0):
    t = x.shape[0]
    n = w.shape[1]
    return pl.pallas_call(
        functools.partial(_inproj_kernel, k0=k0),
        grid=(t // TM, n // tn),
        in_specs=[
            pl.BlockSpec((TM, D_MODEL), lambda i, j: (i, 0)),
            _mod_spec(row_of_tile, TM),
            pl.BlockSpec((1, D_MODEL), lambda i, j: (0, 0)),
            pl.BlockSpec((D_MODEL, tn), lambda i, j: (0, j)),
            pl.BlockSpec((1, tn), lambda i, j: (0, j)),
        ],
        out_specs=pl.BlockSpec((TM, tn), lambda i, j: (i, j)),
        out_shape=jax.ShapeDtypeStruct((t, n), out_dtype),
        scratch_shapes=[pltpu.VMEM((TM, D_MODEL), BF16)],
        compiler_params=_params(("parallel", "arbitrary")),
        name="inproj",
    )(x, mod, g.reshape(1, D_MODEL), w, col_scale)


def _q_kernel(cq_ref, g_ref, wa_ref, wb_ref, cos_ref, sin_ref, o_ref):
    xn = _rms(cq_ref[...], g_ref[...]).astype(BF16)
    ya = _dot(xn, wa_ref[...])
    yb = _dot(xn, wb_ref[...])
    cos, sin = cos_ref[...], sin_ref[...]
    for h in range(MLA_HEADS):
        lo = h * QK_PAD
        o_ref[:, lo:lo + LANES] = (ya[:, lo:lo + LANES] * MLA_SCALE).astype(BF16)
        rope = ya[:, lo + LANES:lo + QK_PAD] * cos + yb[:, h * LANES:(h + 1) * LANES] * sin
        o_ref[:, lo + LANES:lo + QK_PAD] = (rope * MLA_SCALE).astype(BF16)


def _q_heads(p1, g, wa, wb, cos_t, sin_t, table_block):
    t = p1.shape[0]
    return pl.pallas_call(
        _q_kernel,
        grid=(t // TM,),
        in_specs=[
            pl.BlockSpec((TM, Q_LORA), lambda i: (i, 0)),
            pl.BlockSpec((1, Q_LORA), lambda i: (0, 0)),
            pl.BlockSpec((Q_LORA, MLA_HEADS * QK_PAD), lambda i: (0, 0)),
            pl.BlockSpec((Q_LORA, MLA_HEADS * LANES), lambda i: (0, 0)),
            pl.BlockSpec((TM, LANES), lambda i: (table_block(i), 0)),
            pl.BlockSpec((TM, LANES), lambda i: (table_block(i), 0)),
        ],
        out_specs=pl.BlockSpec((TM, MLA_HEADS * QK_PAD), lambda i: (i, 0)),
        out_shape=jax.ShapeDtypeStruct((t, MLA_HEADS * QK_PAD), BF16),
        compiler_params=_params(("parallel",)),
        name="mla_q",
    )(p1, g.reshape(1, Q_LORA), wa, wb, cos_t, sin_t)


def _kv_kernel(ckv_ref, kra_ref, krb_ref, g_ref, wk_ref, wv_ref, cos_ref, sin_ref, k_ref, v_ref):
    xn = _rms(ckv_ref[...], g_ref[...]).astype(BF16)
    yk = _dot(xn, wk_ref[...])
    v_ref[...] = _dot(xn, wv_ref[...]).astype(BF16)
    kr = (kra_ref[...] * cos_ref[...] + krb_ref[...] * sin_ref[...]).astype(BF16)
    for h in range(MLA_HEADS):
        lo = h * QK_PAD
        k_ref[:, lo:lo + LANES] = yk[:, h * LANES:(h + 1) * LANES].astype(BF16)
        k_ref[:, lo + LANES:lo + QK_PAD] = kr


def _kv_heads(p1, g, wk, wv, cos_t, sin_t, table_block):
    t = p1.shape[0]
    kr_col = 2 * Q_LORA // LANES
    return pl.pallas_call(
        _kv_kernel,
        grid=(t // TM,),
        in_specs=[
            pl.BlockSpec((TM, KV_LORA), lambda i: (i, 1)),
            pl.BlockSpec((TM, LANES), lambda i: (i, kr_col)),
            pl.BlockSpec((TM, LANES), lambda i: (i, kr_col + 1)),
            pl.BlockSpec((1, KV_LORA), lambda i: (0, 0)),
            pl.BlockSpec((KV_LORA, MLA_HEADS * QK_NOPE), lambda i: (0, 0)),
            pl.BlockSpec((KV_LORA, D_MLA_OUT), lambda i: (0, 0)),
            pl.BlockSpec((TM, LANES), lambda i: (table_block(i), 0)),
            pl.BlockSpec((TM, LANES), lambda i: (table_block(i), 0)),
        ],
        out_specs=[
            pl.BlockSpec((TM, MLA_HEADS * QK_PAD), lambda i: (i, 0)),
            pl.BlockSpec((TM, D_MLA_OUT), lambda i: (i, 0)),
        ],
        out_shape=[
            jax.ShapeDtypeStruct((t, MLA_HEADS * QK_PAD), BF16),
            jax.ShapeDtypeStruct((t, D_MLA_OUT), BF16),
        ],
        compiler_params=_params(("parallel",)),
        name="mla_kv",
    )(p1, p1, p1, g.reshape(1, KV_LORA), wk, wv, cos_t, sin_t)


def _softmax_pv(s_parts, v_parts):
    m = functools.reduce(jnp.maximum, [jnp.max(s, axis=-1, keepdims=True) for s in s_parts])
    p_parts = [jnp.exp(s - m) for s in s_parts]
    l = functools.reduce(jnp.add, [jnp.sum(p, axis=-1, keepdims=True) for p in p_parts])
    o = functools.reduce(jnp.add, [_dot(p.astype(BF16), v) for p, v in zip(p_parts, v_parts)])
    return o * (1.0 / l)


def _mla_kernel(q_ref, kl_ref, kc_ref, vl_ref, vc_ref, o_ref):
    q = q_ref[...]
    o_ref[...] = _softmax_pv([_dot_nt(q, kl_ref[...]), _dot_nt(q, kc_ref[...])], [vl_ref[...], vc_ref[...]])


def _mla_latent(q, k_lat, k_ctx, v_lat, v_ctx):
    nq = SEQ // TQ
    return pl.pallas_call(
        _mla_kernel,
        grid=(BATCH, MLA_HEADS, nq),
        in_specs=[
            pl.BlockSpec((TQ, QK_PAD), lambda b, h, i: (b * nq + i, h)),
            pl.BlockSpec((SEQ, QK_PAD), lambda b, h, i: (b, h)),
            pl.BlockSpec((CTX_LEN, QK_PAD), lambda b, h, i: (b, h)),
            pl.BlockSpec((SEQ, V_HEAD), lambda b, h, i: (b, h)),
            pl.BlockSpec((CTX_LEN, V_HEAD), lambda b, h, i: (b, h)),
        ],
        out_specs=pl.BlockSpec((TQ, V_HEAD), lambda b, h, i: (b * nq + i, h)),
        out_shape=jax.ShapeDtypeStruct((BATCH * SEQ, D_MLA_OUT), F32),
        compiler_params=_params(("parallel", "parallel", "arbitrary")),
        name="mla_latent",
    )(q, k_lat, k_ctx, v_lat, v_ctx)


def _ctx_attn_kernel(q_ref, k_ref, v_ref, o_ref):
    o_ref[...] = _softmax_pv([_dot_nt(q_ref[...], k_ref[...])], [v_ref[...]])


def _ctx_attention(q, k, v, width, q_off, k_off, v_off):
    return pl.pallas_call(
        _ctx_attn_kernel,
        grid=(BATCH, MLA_HEADS),
        in_specs=[
            pl.BlockSpec((CTX_LEN, width), lambda b, h: (b, q_off + h)),
            pl.BlockSpec((CTX_LEN, width), lambda b, h: (b, k_off + h)),
            pl.BlockSpec((CTX_LEN, V_HEAD), lambda b, h: (b, v_off + h)),
        ],
        out_specs=pl.BlockSpec((CTX_LEN, V_HEAD), lambda b, h: (b, h)),
        out_shape=jax.ShapeDtypeStruct((BATCH * CTX_LEN, MLA_HEADS * V_HEAD), F32),
        compiler_params=_params(("parallel", "parallel")),
        name="ctx_attention",
    )(q, k, v)


def _na_group_window(g):
    if g == 0:
        return 0, 0
    if g == 1:
        return 1, 0
    if g == N_NA_GROUPS - 1:
        return 3, GRID_H - NA_WIN_ROWS
    return 2, NA_GROUP_ROWS * g - WIN_H // 2


def _na_kernel(q_ref, k_ref, v_ref, kc_ref, vc_ref, tab_ref, o_ref):
    kc, vc = kc_ref[...], vc_ref[...]
    gq = NA_GROUP_ROWS * GRID_W
    gk = NA_WIN_ROWS * GRID_W
    for g in range(N_NA_GROUPS):
        var, ks = _na_group_window(g)
        q = q_ref[g * gq:(g + 1) * gq, :]
        kw = k_ref[ks * GRID_W:ks * GRID_W + gk, :]
        vw = v_ref[ks * GRID_W:ks * GRID_W + gk, :]
        s_lat = _dot_nt(q, kw) + tab_ref[var]
        o_ref[g * gq:(g + 1) * gq, :] = _softmax_pv([s_lat, _dot_nt(q, kc)], [vw, vc])


def _na_tables(rpb):
    nh, nr, nc = rpb.shape
    v = jnp.pad(rpb.astype(F32), ((0, 0), (0, 0), (GRID_W - WIN_W, GRID_W + WIN_W - nc)))
    flat = jnp.broadcast_to(v[:, :, None, :], (nh, nr, GRID_W, 2 * GRID_W)).reshape(nh, nr, 2 * GRID_W * GRID_W)
    toe = flat[:, :, :GRID_W * (2 * GRID_W - 1)].reshape(nh, nr, GRID_W, 2 * GRID_W - 1)[..., GRID_W - 1:]
    row_pad = WIN_H // 2
    toe = jnp.pad(toe, ((0, 0), (row_pad, row_pad), (0, 0), (0, 0)))

    qa = np.arange(NA_GROUP_ROWS).reshape(-1, 1, 1, 1)
    qc = np.arange(GRID_W).reshape(1, -1, 1, 1)
    ki = np.arange(NA_WIN_ROWS).reshape(1, 1, -1, 1)
    kc = np.arange(GRID_W).reshape(1, 1, 1, -1)
    col_start = np.clip(qc - WIN_W // 2, 0, GRID_W - WIN_W)
    col_ok = (kc >= col_start) & (kc < col_start + WIN_W)
    tables = []
    for g in (0, 1, 2, N_NA_GROUPS - 1):
        _, ks = _na_group_window(g)
        r = NA_GROUP_ROWS * g + qa
        rs = np.clip(r - WIN_H // 2, 0, GRID_H - WIN_H)
        krow = ks + ki
        ok = np.broadcast_to((krow >= rs) & (krow < rs + WIN_H) & col_ok,
                             (NA_GROUP_ROWS, GRID_W, NA_WIN_ROWS, GRID_W))
        per_row = []
        for a in range(NA_GROUP_ROWS):
            base = ks - (NA_GROUP_ROWS * g + a) + WIN_H - 1 + row_pad
            assert 0 <= base and base + NA_WIN_ROWS <= nr + 2 * row_pad
            per_row.append(toe[:, base:base + NA_WIN_ROWS])
        bias = jnp.stack(per_row, axis=1).transpose(0, 1, 3, 2, 4)
        bias = jnp.where(ok[None], bias, NEG)
        tables.append(bias.reshape(nh, NA_GROUP_ROWS * GRID_W, NA_WIN_ROWS * GRID_W))
    return jnp.stack(tables)


def _na_latent(p2_lat, p2_ctx, tables):
    nh = NA_HEADS
    gq = NA_GROUP_ROWS * GRID_W
    gk = NA_WIN_ROWS * GRID_W
    return pl.pallas_call(
        _na_kernel,
        grid=(nh, BATCH),
        in_specs=[
            pl.BlockSpec((SEQ, NA_HEAD_DIM), lambda h, b: (b, h)),
            pl.BlockSpec((SEQ, NA_HEAD_DIM), lambda h, b: (b, nh + h)),
            pl.BlockSpec((SEQ, NA_HEAD_DIM), lambda h, b: (b, 2 * nh + h)),
            pl.BlockSpec((CTX_LEN, NA_HEAD_DIM), lambda h, b: (b, nh + h)),
            pl.BlockSpec((CTX_LEN, NA_HEAD_DIM), lambda h, b: (b, 2 * nh + h)),
            pl.BlockSpec((4, None, gq, gk), lambda h, b: (0, h, 0, 0)),
        ],
        out_specs=pl.BlockSpec((SEQ, NA_HEAD_DIM), lambda h, b: (b, h)),
        out_shape=jax.ShapeDtypeStruct((BATCH * SEQ, D_NA), F32),
        compiler_params=_params(("parallel", "parallel")),
        name="na_latent",
    )(p2_lat, p2_lat, p2_lat, p2_ctx, p2_ctx, tables)


def _merge_kernel(a_ref, b_ref, h_ref, mod_ref, ga_ref, gb_ref, wa_ref, wb_ref, o_ref, *, k0):
    an = _rms(a_ref[...], ga_ref[...]).astype(BF16)
    bn = _rms(b_ref[...], gb_ref[...]).astype(BF16)
    y = _dot(an, wa_ref[...]) + _dot(bn, wb_ref[...])
    o_ref[...] = h_ref[...] + mod_ref[0, k0:k0 + 1, :] * y


def _merge(a, b, h, mod, row_of_tile, out_norm, w_out, k0):
    t = h.shape[0]
    g2 = out_norm.reshape(2, 1, D_MLA_OUT)
    return pl.pallas_call(
        functools.partial(_merge_kernel, k0=k0),
        grid=(t // TM,),
        in_specs=[
            pl.BlockSpec((TM, D_MLA_OUT), lambda i: (i, 0)),
            pl.BlockSpec((TM, D_NA), lambda i: (i, 0)),
            pl.BlockSpec((TM, D_MODEL), lambda i: (i, 0)),
            _mod_spec(row_of_tile, TM),
            pl.BlockSpec((None, 1, D_MLA_OUT), lambda i: (0, 0, 0)),
            pl.BlockSpec((None, 1, D_NA), lambda i: (1, 0, 0)),
            pl.BlockSpec((D_MLA_OUT, D_MODEL), lambda i: (0, 0)),
            pl.BlockSpec((D_NA, D_MODEL), lambda i: (1, 0)),
        ],
        out_specs=pl.BlockSpec((TM, D_MODEL), lambda i: (i, 0)),
        out_shape=jax.ShapeDtypeStruct((t, D_MODEL), F32),
        compiler_params=_params(("parallel",)),
        name="merge",
    )(a, b, h, mod, g2, g2, w_out, w_out)


def _rope_tables():
    t = np.arange(SEQ)
    pos = np.stack([t // GRID_W, t % GRID_W], axis=-1).astype(np.float32)
    inv_freq = jnp.asarray(ROPE_THETA, F32) ** (-jnp.arange(N_FREQ, dtype=F32) / N_FREQ)
    ang = jnp.asarray(pos)[:, :, None] * inv_freq
    cos, sin = jnp.cos(ang), jnp.sin(ang)
    cos_t = jnp.concatenate([cos, cos], axis=-1).reshape(SEQ, QK_ROPE)
    sin_t = jnp.concatenate([-sin, sin], axis=-1).reshape(SEQ, QK_ROPE)
    pad = jnp.zeros((SEQ, LANES - QK_ROPE), F32)
    cos_t = jnp.concatenate([cos_t, pad], axis=-1)
    sin_t = jnp.concatenate([sin_t, pad], axis=-1)
    ident = jnp.concatenate([jnp.ones((TM, QK_ROPE), F32), jnp.zeros((TM, LANES - QK_ROPE), F32)], axis=-1)
    return (jnp.concatenate([cos_t, ident], axis=0),
            jnp.concatenate([sin_t, jnp.zeros((TM, LANES), F32)], axis=0))


def _rotate_half_cols(w):
    blocks = w.reshape(w.shape[:-1] + (2, 2, N_FREQ))
    return blocks[..., ::-1, :].reshape(w.shape)


def _layer_weights(w_in, w_uq, w_ukv):
    kr = w_in[:, 2 * Q_LORA:2 * Q_LORA + QK_ROPE]
    zpad = jnp.zeros((D_MODEL, LANES - QK_ROPE), F32)
    w_p1 = jnp.concatenate([w_in[:, :2 * Q_LORA], kr, zpad, _rotate_half_cols(kr), zpad], axis=1).astype(BF16)
    w_p2 = w_in[:, 2 * Q_LORA + QK_ROPE:].astype(BF16)
    uq = w_uq.reshape(Q_LORA, MLA_HEADS, QK_NOPE + QK_ROPE)
    hz = jnp.zeros((Q_LORA, MLA_HEADS, QK_PAD - QK_NOPE - QK_ROPE), F32)
    wa = jnp.concatenate([uq, hz], axis=-1).reshape(Q_LORA, MLA_HEADS * QK_PAD).astype(BF16)
    wb = jnp.concatenate([_rotate_half_cols(uq[..., QK_NOPE:]), hz], axis=-1)
    wb = wb.reshape(Q_LORA, MLA_HEADS * LANES).astype(BF16)
    ukv = w_ukv.reshape(KV_LORA, MLA_HEADS, QK_NOPE + V_HEAD)
    wk = ukv[..., :QK_NOPE].reshape(KV_LORA, MLA_HEADS * QK_NOPE).astype(BF16)
    wv = ukv[..., QK_NOPE:].reshape(KV_LORA, D_MLA_OUT).astype(BF16)
    return w_p1, w_p2, wa, wb, wk, wv


def kernel(x, c, ctx, c_ctx, ada_w, ada_b, ffn1_norm, ffn1_w1, ffn1_w3, ffn1_w2, mix_norm, w_in, qa_norm, w_uq,
           kva_norm, w_ukv, na_rpb, out_norm, w_out, ffn2_norm, ffn2_w1, ffn2_w3, ffn2_w2, final_norm):
    c_rows = jnp.concatenate([c, c_ctx[None, :], jnp.zeros((MOD_ROWS - BATCH - 1, D_MODEL), F32)], axis=0)
    mod = _modulation(c_rows, ada_w, ada_b)
    cos_t, sin_t = _rope_tables()
    ones_p1 = jnp.ones((1, P1_COLS), F32)
    p2_scale = jnp.concatenate([jnp.full((1, D_NA), NA_SCALE, F32), jnp.ones((1, 2 * D_NA), F32)], axis=1)
    tiles_per_sample = SEQ // TM

    h = x.reshape(BATCH * SEQ, D_MODEL)
    hc = ctx.reshape(BATCH * CTX_LEN, D_MODEL)
    for l in range(DEPTH):
        last = l == DEPTH - 1
        lat_row = lambda i, tm, l=l: l * MOD_ROWS + (i * tm) // SEQ
        ctx_row = lambda i, tm, l=l: l * MOD_ROWS + CTX_MOD_ROW
        lat_tab = lambda i: i % tiles_per_sample
        ctx_tab = lambda i: tiles_per_sample
        w1, w3, w2 = ffn1_w1[l].astype(BF16), ffn1_w3[l].astype(BF16), ffn1_w2[l].astype(BF16)
        h = _ffn(h, mod, lat_row, ffn1_norm[l], w1, w3, w2, 0)
        hc = _ffn(hc, mod, ctx_row, ffn1_norm[l], w1, w3, w2, 0)

        w_p1, w_p2, wa, wb, wk, wv = _layer_weights(w_in[l], w_uq[l], w_ukv[l])
        tables = _na_tables(na_rpb[l])
        streams = []
        for hs, row, tab in ((h, lat_row, lat_tab), (hc, ctx_row, ctx_tab)):
            p1 = _inproj(hs, mod, row, mix_norm[l], w_p1, ones_p1, P1_COLS, F32, 3)
            p2 = _inproj(hs, mod, row, mix_norm[l], w_p2, p2_scale, 1024, BF16, 3)
            k, v = _kv_heads(p1, kva_norm[l], wk, wv, cos_t, sin_t, tab)
            streams.append((p1, p2, k, v, row, tab))
        (p1, p2, k, v, _, _), (p1c, p2c, kc, vc, _, _) = streams

        q = _q_heads(p1, qa_norm[l], wa, wb, cos_t, sin_t, lat_tab)
        a_lat = _mla_latent(q, k, kc, v, vc)
        b_lat = _na_latent(p2, p2c, tables)
        h = _merge(a_lat, b_lat, h, mod, lat_row, out_norm[l], w_out[l].astype(BF16), 5)
        w1, w3, w2 = ffn2_w1[l].astype(BF16), ffn2_w3[l].astype(BF16), ffn2_w2[l].astype(BF16)
        h = _ffn(h, mod, lat_row, ffn2_norm[l], w1, w3, w2, 6, final_g=final_norm if last else None)
        if not last:
            qc = _q_heads(p1c, qa_norm[l], wa, wb, cos_t, sin_t, ctx_tab)
            a_ctx = _ctx_attention(qc, kc, vc, QK_PAD, 0, 0, 0)
            b_ctx = _ctx_attention(p2c, p2c, p2c, NA_HEAD_DIM, 0, NA_HEADS, 2 * NA_HEADS)
            hc = _merge(a_ctx, b_ctx, hc, mod, ctx_row, out_norm[l], w_out[l].astype(BF16), 5)
            hc = _ffn(hc, mod, ctx_row, ffn2_norm[l], w1, w3, w2, 6)
    return h.reshape(BATCH, SEQ, D_MODEL)
```

```python
import functools

import numpy as np
import jax
import jax.numpy as jnp
from jax import lax
from jax.experimental import pallas as pl
from jax.experimental.pallas import tpu as pltpu

D_MODEL = 2048
BATCH = 16
SEQ = 2048
DEPTH = 2
GRID_W = 64
GRID_H = SEQ // GRID_W
CTX_LEN = 256
MLA_HEADS = 8
QK_NOPE = 128
QK_ROPE = 64
V_HEAD = 128
Q_LORA = 512
KV_LORA = 512
N_FREQ = QK_ROPE // 4
ROPE_THETA = 10000.0
MLA_SCALE = (QK_NOPE + QK_ROPE) ** -0.5
NA_HEADS = 8
NA_HEAD_DIM = 128
WIN_H = 8
WIN_W = 16
NA_SCALE = NA_HEAD_DIM ** -0.5
D_MLA_OUT = MLA_HEADS * V_HEAD
D_NA = NA_HEADS * NA_HEAD_DIM
D_FF = 5632
N_MOD = 9
EPS = 1e-6
NEG = -1e30

F32 = jnp.float32
BF16 = jnp.bfloat16

LANES = 128
V7X_VMEM_LIMIT = 56 * 1024 * 1024

MOD_ROWS = 24
CTX_MOD_ROW = BATCH
QK_PAD = 256
P1_COLS = 2 * Q_LORA + 2 * LANES
NA_GROUP_ROWS = 4
NA_WIN_ROWS = 12
N_NA_GROUPS = GRID_H // NA_GROUP_ROWS

TM = 512
FFN_TM = 1024
FFN_MC = 512
FFN_NC = 512
TF = 512
TQ = 1024


def _params(sem, vmem=V7X_VMEM_LIMIT):
    return pltpu.CompilerParams(dimension_semantics=sem, vmem_limit_bytes=vmem)


def _rms(x, g):
    return x * lax.rsqrt(jnp.mean(x * x, axis=-1, keepdims=True) + EPS) * g


def _silu(a):
    return a * (1.0 / (1.0 + jnp.exp(-a)))


def _dot(a, b):
    return jnp.dot(a, b, preferred_element_type=F32)


def _dot_nt(a, b):
    return lax.dot_general(a, b, (((1,), (1,)), ((), ())), preferred_element_type=F32)


def _mod_kernel(s_ref, w_ref, b_ref, o_ref):
    s = _silu(s_ref[...])
    o_ref[...] = _dot(s.astype(BF16), w_ref[...].astype(BF16)) + b_ref[...]


def _modulation(c_rows, ada_w, ada_b):
    tn = 1024
    n = N_MOD * D_MODEL
    out = pl.pallas_call(
        _mod_kernel,
        grid=(DEPTH, n // tn),
        in_specs=[
            pl.BlockSpec((MOD_ROWS, D_MODEL), lambda l, j: (0, 0)),
            pl.BlockSpec((None, D_MODEL, tn), lambda l, j: (l, 0, j)),
            pl.BlockSpec((None, 1, tn), lambda l, j: (l, 0, j)),
        ],
        out_specs=pl.BlockSpec((None, MOD_ROWS, tn), lambda l, j: (l, 0, j)),
        out_shape=jax.ShapeDtypeStruct((DEPTH, MOD_ROWS, n), F32),
        compiler_params=_params(("parallel", "parallel")),
        name="adaln_mod",
    )(c_rows, ada_w, ada_b.reshape(DEPTH, 1, n))
    return out.reshape(DEPTH * MOD_ROWS, N_MOD, D_MODEL)


def _mod_spec(row_of_tile, tm):
    return pl.BlockSpec((1, N_MOD, D_MODEL), lambda i, *_: (row_of_tile(i, tm), 0, 0))


def _ffn_kernel(x_ref, mod_ref, g_ref, w1_ref, w3_ref, w2_ref, gf_ref, o_ref, xn_ref, *, k0, final):
    j = pl.program_id(1)

    @pl.when(j == 0)
    def _():
        y = _rms(x_ref[...], g_ref[...])
        xn_ref[...] = (y * (1.0 + mod_ref[0, k0 + 1:k0 + 2, :]) + mod_ref[0, k0:k0 + 1, :]).astype(BF16)
        o_ref[...] = jnp.zeros_like(o_ref)

    for m in range(FFN_TM // FFN_MC):
        rows = slice(m * FFN_MC, (m + 1) * FFN_MC)
        xn = xn_ref[rows, :]
        a = _dot(xn, w1_ref[...])
        b = _dot(xn, w3_ref[...])
        gated = (_silu(a) * b).astype(BF16)
        for n in range(D_MODEL // FFN_NC):
            cols = slice(n * FFN_NC, (n + 1) * FFN_NC)
            o_ref[rows, cols] += _dot(gated, w2_ref[:, cols])

    @pl.when(j == pl.num_programs(1) - 1)
    def _():
        h = x_ref[...] + (0.5 * mod_ref[0, k0 + 2:k0 + 3, :]) * o_ref[...]
        o_ref[...] = _rms(h, gf_ref[...]) if final else h


def _ffn(x, mod, row_of_tile, g, w1, w3, w2, k0, final_g=None):
    t = x.shape[0]
    final = final_g is not None
    gf = final_g if final else g
    return pl.pallas_call(
        functools.partial(_ffn_kernel, k0=k0, final=final),
        grid=(t // FFN_TM, D_FF // TF),
        in_specs=[
            pl.BlockSpec((FFN_TM, D_MODEL), lambda i, j: (i, 0), pipeline_mode=pl.Buffered(1)),
            _mod_spec(row_of_tile, FFN_TM),
            pl.BlockSpec((1, D_MODEL), lambda i, j: (0, 0)),
            pl.BlockSpec((D_MODEL, TF), lambda i, j: (0, j)),
            pl.BlockSpec((D_MODEL, TF), lambda i, j: (0, j)),
            pl.BlockSpec((TF, D_MODEL), lambda i, j: (j, 0)),
            pl.BlockSpec((1, D_MODEL), lambda i, j: (0, 0)),
        ],
        out_specs=pl.BlockSpec((FFN_TM, D_MODEL), lambda i, j: (i, 0)),
        out_shape=jax.ShapeDtypeStruct((t, D_MODEL), F32),
        scratch_shapes=[pltpu.VMEM((FFN_TM, D_MODEL), BF16)],
        compiler_params=_params(("parallel", "arbitrary")),
        name="ffn",
    )(x, mod, g.reshape(1, D_MODEL), w1, w3, w2, gf.reshape(1, D_MODEL))


def _inproj_kernel(x_ref, mod_ref, g_ref, w_ref, cs_ref, o_ref, xn_ref, *, k0):
    @pl.when(pl.program_id(1) == 0)
    def _():
        y = _rms(x_ref[...], g_ref[...])
        xn_ref[...] = (y * (1.0 + mod_ref[0, k0 + 1:k0 + 2, :]) + mod_ref[0, k0:k0 + 1, :]).astype(BF16)

    o_ref[...] = (_dot(xn_ref[...], w_ref[...]) * cs_ref[...]).astype(o_ref.dtype)


def _inproj(x, mod, row_of_tile, g, w, col_scale, tn, out_dtype, k0):
    t = x.shape[0]
    n = w.shape[1]
    return pl.pallas_call(
        functools.partial(_inproj_kernel, k0=k0),
        grid=(t // TM, n // tn),
        in_specs=[
            pl.BlockSpec((TM, D_MODEL), lambda i, j: (i, 0)),
            _mod_spec(row_of_tile, TM),
            pl.BlockSpec((1, D_MODEL), lambda i, j: (0, 0)),
            pl.BlockSpec((D_MODEL, tn), lambda i, j: (0, j)),
            pl.BlockSpec((1, tn), lambda i, j: (0, j)),
        ],
        out_specs=pl.BlockSpec((TM, tn), lambda i, j: (i, j)),
        out_shape=jax.ShapeDtypeStruct((t, n), out_dtype),
        scratch_shapes=[pltpu.VMEM((TM, D_MODEL), BF16)],
        compiler_params=_params(("parallel", "arbitrary")),
        name="inproj",
    )(x, mod, g.reshape(1, D_MODEL), w, col_scale)


def _q_kernel(cq_ref, g_ref, wa_ref, wb_ref, cos_ref, sin_ref, o_ref):
    xn = _rms(cq_ref[...], g_ref[...]).astype(BF16)
    ya = _dot(xn, wa_ref[...])
    yb = _dot(xn, wb_ref[...])
    cos, sin = cos_ref[...], sin_ref[...]
    for h in range(MLA_HEADS):
        lo = h * QK_PAD
        o_ref[:, lo:lo + LANES] = (ya[:, lo:lo + LANES] * MLA_SCALE).astype(BF16)
        rope = ya[:, lo + LANES:lo + QK_PAD] * cos + yb[:, h * LANES:(h + 1) * LANES] * sin
        o_ref[:, lo + LANES:lo + QK_PAD] = (rope * MLA_SCALE).astype(BF16)


def _q_heads(p1, g, wa, wb, cos_t, sin_t, table_block):
    t = p1.shape[0]
    return pl.pallas_call(
        _q_kernel,
        grid=(t // TM,),
        in_specs=[
            pl.BlockSpec((TM, Q_LORA), lambda i: (i, 0)),
            pl.BlockSpec((1, Q_LORA), lambda i: (0, 0)),
            pl.BlockSpec((Q_LORA, MLA_HEADS * QK_PAD), lambda i: (0, 0)),
            pl.BlockSpec((Q_LORA, MLA_HEADS * LANES), lambda i: (0, 0)),
            pl.BlockSpec((TM, LANES), lambda i: (table_block(i), 0)),
            pl.BlockSpec((TM, LANES), lambda i: (table_block(i), 0)),
        ],
        out_specs=pl.BlockSpec((TM, MLA_HEADS * QK_PAD), lambda i: (i, 0)),
        out_shape=jax.ShapeDtypeStruct((t, MLA_HEADS * QK_PAD), BF16),
        compiler_params=_params(("parallel",)),
        name="mla_q",
    )(p1, g.reshape(1, Q_LORA), wa, wb, cos_t, sin_t)


def _kv_kernel(ckv_ref, kra_ref, krb_ref, g_ref, wk_ref, wv_ref, cos_ref, sin_ref, k_ref, v_ref):
    xn = _rms(ckv_ref[...], g_ref[...]).astype(BF16)
    yk = _dot(xn, wk_ref[...])
    v_ref[...] = _dot(xn, wv_ref[...]).astype(BF16)
    kr = (kra_ref[...] * cos_ref[...] + krb_ref[...] * sin_ref[...]).astype(BF16)
    for h in range(MLA_HEADS):
        lo = h * QK_PAD
        k_ref[:, lo:lo + LANES] = yk[:, h * LANES:(h + 1) * LANES].astype(BF16)
        k_ref[:, lo + LANES:lo + QK_PAD] = kr


def _kv_heads(p1, g, wk, wv, cos_t, sin_t, table_block):
    t = p1.shape[0]
    kr_col = 2 * Q_LORA // LANES
    return pl.pallas_call(
        _kv_kernel,
        grid=(t // TM,),
        in_specs=[
            pl.BlockSpec((TM, KV_LORA), lambda i: (i, 1)),
            pl.BlockSpec((TM, LANES), lambda i: (i, kr_col)),
            pl.BlockSpec((TM, LANES), lambda i: (i, kr_col + 1)),
            pl.BlockSpec((1, KV_LORA), lambda i: (0, 0)),
            pl.BlockSpec((KV_LORA, MLA_HEADS * QK_NOPE), lambda i: (0, 0)),
            pl.BlockSpec((KV_LORA, D_MLA_OUT), lambda i: (0, 0)),
            pl.BlockSpec((TM, LANES), lambda i: (table_block(i), 0)),
            pl.BlockSpec((TM, LANES), lambda i: (table_block(i), 0)),
        ],
        out_specs=[
            pl.BlockSpec((TM, MLA_HEADS * QK_PAD), lambda i: (i, 0)),
            pl.BlockSpec((TM, D_MLA_OUT), lambda i: (i, 0)),
        ],
        out_shape=[
            jax.ShapeDtypeStruct((t, MLA_HEADS * QK_PAD), BF16),
            jax.ShapeDtypeStruct((t, D_MLA_OUT), BF16),
        ],
        compiler_params=_params(("parallel",)),
        name="mla_kv",
    )(p1, p1, p1, g.reshape(1, KV_LORA), wk, wv, cos_t, sin_t)


def _softmax_pv(s_parts, v_parts):
    m = functools.reduce(jnp.maximum, [jnp.max(s, axis=-1, keepdims=True) for s in s_parts])
    p_parts = [jnp.exp(s - m) for s in s_parts]
    l = functools.reduce(jnp.add, [jnp.sum(p, axis=-1, keepdims=True) for p in p_parts])
    o = functools.reduce(jnp.add, [_dot(p.astype(BF16), v) for p, v in zip(p_parts, v_parts)])
    return o * (1.0 / l)


def _mla_kernel(q_ref, kl_ref, kc_ref, vl_ref, vc_ref, o_ref):
    q = q_ref[...]
    o_ref[...] = _softmax_pv([_dot_nt(q, kl_ref[...]), _dot_nt(q, kc_ref[...])], [vl_ref[...], vc_ref[...]])


def _mla_latent(q, k_lat, k_ctx, v_lat, v_ctx):
    nq = SEQ // TQ
    return pl.pallas_call(
        _mla_kernel,
        grid=(BATCH, MLA_HEADS, nq),
        in_specs=[
            pl.BlockSpec((TQ, QK_PAD), lambda b, h, i: (b * nq + i, h)),
            pl.BlockSpec((SEQ, QK_PAD), lambda b, h, i: (b, h)),
            pl.BlockSpec((CTX_LEN, QK_PAD), lambda b, h, i: (b, h)),
            pl.BlockSpec((SEQ, V_HEAD), lambda b, h, i: (b, h)),
            pl.BlockSpec((CTX_LEN, V_HEAD), lambda b, h, i: (b, h)),
        ],
        out_specs=pl.BlockSpec((TQ, V_HEAD), lambda b, h, i: (b * nq + i, h)),
        out_shape=jax.ShapeDtypeStruct((BATCH * SEQ, D_MLA_OUT), F32),
        compiler_params=_params(("parallel", "parallel", "arbitrary")),
        name="mla_latent",
    )(q, k_lat, k_ctx, v_lat, v_ctx)


def _ctx_attn_kernel(q_ref, k_ref, v_ref, o_ref):
    o_ref[...] = _softmax_pv([_dot_nt(q_ref[...], k_ref[...])], [v_ref[...]])


def _ctx_attention(q, k, v, width, q_off, k_off, v_off):
    return pl.pallas_call(
        _ctx_attn_kernel,
        grid=(BATCH, MLA_HEADS),
        in_specs=[
            pl.BlockSpec((CTX_LEN, width), lambda b, h: (b, q_off + h)),
            pl.BlockSpec((CTX_LEN, width), lambda b, h: (b, k_off + h)),
            pl.BlockSpec((CTX_LEN, V_HEAD), lambda b, h: (b, v_off + h)),
        ],
        out_specs=pl.BlockSpec((CTX_LEN, V_HEAD), lambda b, h: (b, h)),
        out_shape=jax.ShapeDtypeStruct((BATCH * CTX_LEN, MLA_HEADS * V_HEAD), F32),
        compiler_params=_params(("parallel", "parallel")),
        name="ctx_attention",
    )(q, k, v)


def _na_group_window(g):
    if g == 0:
        return 0, 0
    if g == 1:
        return 1, 0
    if g == N_NA_GROUPS - 1:
        return 3, GRID_H - NA_WIN_ROWS
    return 2, NA_GROUP_ROWS * g - WIN_H // 2


def _na_kernel(q_ref, k_ref, v_ref, kc_ref, vc_ref, tab_ref, o_ref):
    kc, vc = kc_ref[...], vc_ref[...]
    gq = NA_GROUP_ROWS * GRID_W
    gk = NA_WIN_ROWS * GRID_W
    for g in range(N_NA_GROUPS):
        var, ks = _na_group_window(g)
        q = q_ref[g * gq:(g + 1) * gq, :]
        kw = k_ref[ks * GRID_W:ks * GRID_W + gk, :]
        vw = v_ref[ks * GRID_W:ks * GRID_W + gk, :]
        s_lat = _dot_nt(q, kw) + tab_ref[var]
        o_ref[g * gq:(g + 1) * gq, :] = _softmax_pv([s_lat, _dot_nt(q, kc)], [vw, vc])


def _na_tables(rpb):
    nh, nr, nc = rpb.shape
    v = jnp.pad(rpb.astype(F32), ((0, 0), (0, 0), (GRID_W - WIN_W, GRID_W + WIN_W - nc)))
    flat = jnp.broadcast_to(v[:, :, None, :], (nh, nr, GRID_W, 2 * GRID_W)).reshape(nh, nr, 2 * GRID_W * GRID_W)
    toe = flat[:, :, :GRID_W * (2 * GRID_W - 1)].reshape(nh, nr, GRID_W, 2 * GRID_W - 1)[..., GRID_W - 1:]
    row_pad = WIN_H // 2
    toe = jnp.pad(toe, ((0, 0), (row_pad, row_pad), (0, 0), (0, 0)))

    qa = np.arange(NA_GROUP_ROWS).reshape(-1, 1, 1, 1)
    qc = np.arange(GRID_W).reshape(1, -1, 1, 1)
    ki = np.arange(NA_WIN_ROWS).reshape(1, 1, -1, 1)
    kc = np.arange(GRID_W).reshape(1, 1, 1, -1)
    col_start = np.clip(qc - WIN_W // 2, 0, GRID_W - WIN_W)
    col_ok = (kc >= col_start) & (kc < col_start + WIN_W)
    tables = []
    for g in (0, 1, 2, N_NA_GROUPS - 1):
        _, ks = _na_group_window(g)
        r = NA_GROUP_ROWS * g + qa
        rs = np.clip(r - WIN_H // 2, 0, GRID_H - WIN_H)
        krow = ks + ki
        ok = np.broadcast_to((krow >= rs) & (krow < rs + WIN_H) & col_ok,
                             (NA_GROUP_ROWS, GRID_W, NA_WIN_ROWS, GRID_W))
        per_row = []
        for a in range(NA_GROUP_ROWS):
            base = ks - (NA_GROUP_ROWS * g + a) + WIN_H - 1 + row_pad
            assert 0 <= base and base + NA_WIN_ROWS <= nr + 2 * row_pad
            per_row.append(toe[:, base:base + NA_WIN_ROWS])
        bias = jnp.stack(per_row, axis=1).transpose(0, 1, 3, 2, 4)
        bias = jnp.where(ok[None], bias, NEG)
        tables.append(bias.reshape(nh, NA_GROUP_ROWS * GRID_W, NA_WIN_ROWS * GRID_W))
    return jnp.stack(tables)


def _na_latent(p2_lat, p2_ctx, tables):
    nh = NA_HEADS
    gq = NA_GROUP_ROWS * GRID_W
    gk = NA_WIN_ROWS * GRID_W
    return pl.pallas_call(
        _na_kernel,
        grid=(nh, BATCH),
        in_specs=[
            pl.BlockSpec((SEQ, NA_HEAD_DIM), lambda h, b: (b, h)),
            pl.BlockSpec((SEQ, NA_HEAD_DIM), lambda h, b: (b, nh + h)),
            pl.BlockSpec((SEQ, NA_HEAD_DIM), lambda h, b: (b, 2 * nh + h)),
            pl.BlockSpec((CTX_LEN, NA_HEAD_DIM), lambda h, b: (b, nh + h)),
            pl.BlockSpec((CTX_LEN, NA_HEAD_DIM), lambda h, b: (b, 2 * nh + h)),
            pl.BlockSpec((4, None, gq, gk), lambda h, b: (0, h, 0, 0)),
        ],
        out_specs=pl.BlockSpec((SEQ, NA_HEAD_DIM), lambda h, b: (b, h)),
        out_shape=jax.ShapeDtypeStruct((BATCH * SEQ, D_NA), F32),
        compiler_params=_params(("parallel", "parallel")),
        name="na_latent",
    )(p2_lat, p2_lat, p2_lat, p2_ctx, p2_ctx, tables)


def _merge_kernel(a_ref, b_ref, h_ref, mod_ref, ga_ref, gb_ref, wa_ref, wb_ref, o_ref, *, k0):
    an = _rms(a_ref[...], ga_ref[...]).astype(BF16)
    bn = _rms(b_ref[...], gb_ref[...]).astype(BF16)
    y = _dot(an, wa_ref[...]) + _dot(bn, wb_ref[...])
    o_ref[...] = h_ref[...] + mod_ref[0, k0:k0 + 1, :] * y


def _merge(a, b, h, mod, row_of_tile, out_norm, w_out, k0):
    t = h.shape[0]
    g2 = out_norm.reshape(2, 1, D_MLA_OUT)
    return pl.pallas_call(
        functools.partial(_merge_kernel, k0=k0),
        grid=(t // TM,),
        in_specs=[
            pl.BlockSpec((TM, D_MLA_OUT), lambda i: (i, 0)),
            pl.BlockSpec((TM, D_NA), lambda i: (i, 0)),
            pl.BlockSpec((TM, D_MODEL), lambda i: (i, 0)),
            _mod_spec(row_of_tile, TM),
            pl.BlockSpec((None, 1, D_MLA_OUT), lambda i: (0, 0, 0)),
            pl.BlockSpec((None, 1, D_NA), lambda i: (1, 0, 0)),
            pl.BlockSpec((D_MLA_OUT, D_MODEL), lambda i: (0, 0)),
            pl.BlockSpec((D_NA, D_MODEL), lambda i: (1, 0)),
        ],
        out_specs=pl.BlockSpec((TM, D_MODEL), lambda i: (i, 0)),
        out_shape=jax.ShapeDtypeStruct((t, D_MODEL), F32),
        compiler_params=_params(("parallel",)),
        name="merge",
    )(a, b, h, mod, g2, g2, w_out, w_out)


def _rope_tables():
    t = np.arange(SEQ)
    pos = np.stack([t // GRID_W, t % GRID_W], axis=-1).astype(np.float32)
    inv_freq = jnp.asarray(ROPE_THETA, F32) ** (-jnp.arange(N_FREQ, dtype=F32) / N_FREQ)
    ang = jnp.asarray(pos)[:, :, None] * inv_freq
    cos, sin = jnp.cos(ang), jnp.sin(ang)
    cos_t = jnp.concatenate([cos, cos], axis=-1).reshape(SEQ, QK_ROPE)
    sin_t = jnp.concatenate([-sin, sin], axis=-1).reshape(SEQ, QK_ROPE)
    pad = jnp.zeros((SEQ, LANES - QK_ROPE), F32)
    cos_t = jnp.concatenate([cos_t, pad], axis=-1)
    sin_t = jnp.concatenate([sin_t, pad], axis=-1)
    ident = jnp.concatenate([jnp.ones((TM, QK_ROPE), F32), jnp.zeros((TM, LANES - QK_ROPE), F32)], axis=-1)
    return (jnp.concatenate([cos_t, ident], axis=0),
            jnp.concatenate([sin_t, jnp.zeros((TM, LANES), F32)], axis=0))


def _rotate_half_cols(w):
    blocks = w.reshape(w.shape[:-1] + (2, 2, N_FREQ))
    return blocks[..., ::-1, :].reshape(w.shape)


def _layer_weights(w_in, w_uq, w_ukv):
    kr = w_in[:, 2 * Q_LORA:2 * Q_LORA + QK_ROPE]
    zpad = jnp.zeros((D_MODEL, LANES - QK_ROPE), F32)
    w_p1 = jnp.concatenate([w_in[:, :2 * Q_LORA], kr, zpad, _rotate_half_cols(kr), zpad], axis=1).astype(BF16)
    w_p2 = w_in[:, 2 * Q_LORA + QK_ROPE:].astype(BF16)
    uq = w_uq.reshape(Q_LORA, MLA_HEADS, QK_NOPE + QK_ROPE)
    hz = jnp.zeros((Q_LORA, MLA_HEADS, QK_PAD - QK_NOPE - QK_ROPE), F32)
    wa = jnp.concatenate([uq, hz], axis=-1).reshape(Q_LORA, MLA_HEADS * QK_PAD).astype(BF16)
    wb = jnp.concatenate([_rotate_half_cols(uq[..., QK_NOPE:]), hz], axis=-1)
    wb = wb.reshape(Q_LORA, MLA_HEADS * LANES).astype(BF16)
    ukv = w_ukv.reshape(KV_LORA, MLA_HEADS, QK_NOPE + V_HEAD)
    wk = ukv[..., :QK_NOPE].reshape(KV_LORA, MLA_HEADS * QK_NOPE).astype(BF16)
    wv = ukv[..., QK_NOPE:].reshape(KV_LORA, D_MLA_OUT).astype(BF16)
    return w_p1, w_p2, wa, wb, wk, wv


def kernel(x, c, ctx, c_ctx, ada_w, ada_b, ffn1_norm, ffn1_w1, ffn1_w3, ffn1_w2, mix_norm, w_in, qa_norm, w_uq,
           kva_norm, w_ukv, na_rpb, out_norm, w_out, ffn2_norm, ffn2_w1, ffn2_w3, ffn2_w2, final_norm):
    c_rows = jnp.concatenate([c, c_ctx[None, :], jnp.zeros((MOD_ROWS - BATCH - 1, D_MODEL), F32)], axis=0)
    mod = _modulation(c_rows, ada_w, ada_b)
    cos_t, sin_t = _rope_tables()
    ones_p1 = jnp.ones((1, P1_COLS), F32)
    p2_scale = jnp.concatenate([jnp.full((1, D_NA), NA_SCALE, F32), jnp.ones((1, 2 * D_NA), F32)], axis=1)
    tiles_per_sample = SEQ // TM

    h = x.reshape(BATCH * SEQ, D_MODEL)
    hc = ctx.reshape(BATCH * CTX_LEN, D_MODEL)
    for l in range(DEPTH):
        last = l == DEPTH - 1
        lat_row = lambda i, tm, l=l: l * MOD_ROWS + (i * tm) // SEQ
        ctx_row = lambda i, tm, l=l: l * MOD_ROWS + CTX_MOD_ROW
        lat_tab = lambda i: i % tiles_per_sample
        ctx_tab = lambda i: tiles_per_sample
        w1, w3, w2 = ffn1_w1[l].astype(BF16), ffn1_w3[l].astype(BF16), ffn1_w2[l].astype(BF16)
        h = _ffn(h, mod, lat_row, ffn1_norm[l], w1, w3, w2, 0)
        hc = _ffn(hc, mod, ctx_row, ffn1_norm[l], w1, w3, w2, 0)

        w_p1, w_p2, wa, wb, wk, wv = _layer_weights(w_in[l], w_uq[l], w_ukv[l])
        tables = _na_tables(na_rpb[l])
        streams = []
        for hs, row, tab in ((h, lat_row, lat_tab), (hc, ctx_row, ctx_tab)):
            p1 = _inproj(hs, mod, row, mix_norm[l], w_p1, ones_p1, P1_COLS, F32, 3)
            p2 = _inproj(hs, mod, row, mix_norm[l], w_p2, p2_scale, 1024, BF16, 3)
            k, v = _kv_heads(p1, kva_norm[l], wk, wv, cos_t, sin_t, tab)
            streams.append((p1, p2, k, v, row, tab))
        (p1, p2, k, v, _, _), (p1c, p2c, kc, vc, _, _) = streams

        q = _q_heads(p1, qa_norm[l], wa, wb, cos_t, sin_t, lat_tab)
        a_lat = _mla_latent(q, k, kc, v, vc)
        b_lat = _na_latent(p2, p2c, tables)
        h = _merge(a_lat, b_lat, h, mod, lat_row, out_norm[l], w_out[l].astype(BF16), 5)
        w1, w3, w2 = ffn2_w1[l].astype(BF16), ffn2_w3[l].astype(BF16), ffn2_w2[l].astype(BF16)
        h = _ffn(h, mod, lat_row, ffn2_norm[l], w1, w3, w2, 6, final_g=final_norm if last else None)
        if not last:
            qc = _q_heads(p1c, qa_norm[l], wa, wb, cos_t, sin_t, ctx_tab)
            a_ctx = _ctx_attention(qc, kc, vc, QK_PAD, 0, 0, 0)
            b_ctx = _ctx_attention(p2c, p2c, p2c, NA_HEAD_DIM, 0, NA_HEADS, 2 * NA_HEADS)
            hc = _merge(a_ctx, b_ctx, hc, mod, ctx_row, out_norm[l], w_out[l].astype(BF16), 5)
            hc = _ffn(hc, mod, ctx_row, ffn2_norm[l], w1, w3, w2, 6)
    return h.reshape(BATCH, SEQ, D_MODEL)
```

```python
import functools

import numpy as np
import jax
import jax.numpy as jnp
from jax import lax
from jax.experimental import pallas as pl
from jax.experimental.pallas import tpu as pltpu

D_MODEL = 2048
BATCH = 16
SEQ = 2048
DEPTH = 2
GRID_W = 64
GRID_H = SEQ // GRID_W
CTX_LEN = 256
MLA_HEADS = 8
QK_NOPE = 128
QK_ROPE = 64
V_HEAD = 128
Q_LORA = 512
KV_LORA = 512
N_FREQ = QK_ROPE // 4
ROPE_THETA = 10000.0
MLA_SCALE = (QK_NOPE + QK_ROPE) ** -0.5
NA_HEADS = 8
NA_HEAD_DIM = 128
WIN_H = 8
WIN_W = 16
NA_SCALE = NA_HEAD_DIM ** -0.5
D_MLA_OUT = MLA_HEADS * V_HEAD
D_NA = NA_HEADS * NA_HEAD_DIM
D_FF = 5632
N_MOD = 9
EPS = 1e-6
NEG = -1e30

F32 = jnp.float32
BF16 = jnp.bfloat16

LANES = 128
V7X_VMEM_LIMIT = 56 * 1024 * 1024

MOD_ROWS = 24
CTX_MOD_ROW = BATCH
QK_PAD = 256
P1_COLS = 2 * Q_LORA + 2 * LANES
NA_GROUP_ROWS = 4
NA_WIN_ROWS = 12
N_NA_GROUPS = GRID_H // NA_GROUP_ROWS

TM = 512
FFN_TM = 1024
FFN_MC = 512
FFN_NC = 512
TF = 512
TQ = 1024
MLA_QC = 256


def _params(sem, vmem=V7X_VMEM_LIMIT):
    return pltpu.CompilerParams(dimension_semantics=sem, vmem_limit_bytes=vmem)


def _rms(x, g):
    return x * lax.rsqrt(jnp.mean(x * x, axis=-1, keepdims=True) + EPS) * g


def _silu(a):
    return a * (1.0 / (1.0 + jnp.exp(-a)))


def _dot(a, b):
    return jnp.dot(a, b, preferred_element_type=F32)


def _dot_nt(a, b):
    return lax.dot_general(a, b, (((1,), (1,)), ((), ())), preferred_element_type=F32)


def _mod_kernel(s_ref, w_ref, b_ref, o_ref):
    s = _silu(s_ref[...])
    o_ref[...] = _dot(s.astype(BF16), w_ref[...].astype(BF16)) + b_ref[...]


def _modulation(c_rows, ada_w, ada_b):
    tn = 1024
    n = N_MOD * D_MODEL
    out = pl.pallas_call(
        _mod_kernel,
        grid=(DEPTH, n // tn),
        in_specs=[
            pl.BlockSpec((MOD_ROWS, D_MODEL), lambda l, j: (0, 0)),
            pl.BlockSpec((None, D_MODEL, tn), lambda l, j: (l, 0, j)),
            pl.BlockSpec((None, 1, tn), lambda l, j: (l, 0, j)),
        ],
        out_specs=pl.BlockSpec((None, MOD_ROWS, tn), lambda l, j: (l, 0, j)),
        out_shape=jax.ShapeDtypeStruct((DEPTH, MOD_ROWS, n), F32),
        compiler_params=_params(("parallel", "parallel")),
        name="adaln_mod",
    )(c_rows, ada_w, ada_b.reshape(DEPTH, 1, n))
    return out.reshape(DEPTH * MOD_ROWS, N_MOD, D_MODEL)


def _mod_spec(row_of_tile, tm):
    return pl.BlockSpec((1, N_MOD, D_MODEL), lambda i, *_: (row_of_tile(i, tm), 0, 0))


def _ffn_kernel(x_ref, mod_ref, g_ref, w1_ref, w3_ref, w2_ref, gf_ref, o_ref, xn_ref, *, k0, final):
    j = pl.program_id(1)

    @pl.when(j == 0)
    def _():
        y = _rms(x_ref[...], g_ref[...])
        xn_ref[...] = (y * (1.0 + mod_ref[0, k0 + 1:k0 + 2, :]) + mod_ref[0, k0:k0 + 1, :]).astype(BF16)
        o_ref[...] = jnp.zeros_like(o_ref)

    for m in range(FFN_TM // FFN_MC):
        rows = slice(m * FFN_MC, (m + 1) * FFN_MC)
        xn = xn_ref[rows, :]
        a = _dot(xn, w1_ref[...])
        b = _dot(xn, w3_ref[...])
        gated = (_silu(a) * b).astype(BF16)
        for n in range(D_MODEL // FFN_NC):
            cols = slice(n * FFN_NC, (n + 1) * FFN_NC)
            o_ref[rows, cols] += _dot(gated, w2_ref[:, cols])

    @pl.when(j == pl.num_programs(1) - 1)
    def _():
        h = x_ref[...] + (0.5 * mod_ref[0, k0 + 2:k0 + 3, :]) * o_ref[...]
        o_ref[...] = _rms(h, gf_ref[...]) if final else h


def _ffn(x, mod, row_of_tile, g, w1, w3, w2, k0, final_g=None):
    t = x.shape[0]
    final = final_g is not None
    gf = final_g if final else g
    return pl.pallas_call(
        functools.partial(_ffn_kernel, k0=k0, final=final),
        grid=(t // FFN_TM, D_FF // TF),
        in_specs=[
            pl.BlockSpec((FFN_TM, D_MODEL), lambda i, j: (i, 0), pipeline_mode=pl.Buffered(1)),
            _mod_spec(row_of_tile, FFN_TM),
            pl.BlockSpec((1, D_MODEL), lambda i, j: (0, 0)),
            pl.BlockSpec((None, D_MODEL, TF), lambda i, j: (j, 0, 0)),
            pl.BlockSpec((None, D_MODEL, TF), lambda i, j: (j, 0, 0)),
            pl.BlockSpec((TF, D_MODEL), lambda i, j: (j, 0)),
            pl.BlockSpec((1, D_MODEL), lambda i, j: (0, 0)),
        ],
        out_specs=pl.BlockSpec((FFN_TM, D_MODEL), lambda i, j: (i, 0)),
        out_shape=jax.ShapeDtypeStruct((t, D_MODEL), F32),
        scratch_shapes=[pltpu.VMEM((FFN_TM, D_MODEL), BF16)],
        compiler_params=_params(("parallel", "arbitrary")),
        name="ffn",
    )(x, mod, g.reshape(1, D_MODEL), w1, w3, w2, gf.reshape(1, D_MODEL))


def _inproj_kernel(x_ref, mod_ref, g_ref, w_ref, cs_ref, o_ref, xn_ref, *, k0):
    @pl.when(pl.program_id(1) == 0)
    def _():
        y = _rms(x_ref[...], g_ref[...])
        xn_ref[...] = (y * (1.0 + mod_ref[0, k0 + 1:k0 + 2, :]) + mod_ref[0, k0:k0 + 1, :]).astype(BF16)

    o_ref[...] = (_dot(xn_ref[...], w_ref[...]) * cs_ref[...]).astype(o_ref.dtype)


def _inproj(x, mod, row_of_tile, g, w, col_scale, tn, out_dtype, k0):
    t = x.shape[0]
    n = w.shape[1]
    return pl.pallas_call(
        functools.partial(_inproj_kernel, k0=k0),
        grid=(t // TM, n // tn),
        in_specs=[
            pl.BlockSpec((TM, D_MODEL), lambda i, j: (i, 0)),
            _mod_spec(row_of_tile, TM),
            pl.BlockSpec((1, D_MODEL), lambda i, j: (0, 0)),
            pl.BlockSpec((D_MODEL, tn), lambda i, j: (0, j)),
            pl.BlockSpec((1, tn), lambda i, j: (0, j)),
        ],
        out_specs=pl.BlockSpec((TM, tn), lambda i, j: (i, j)),
        out_shape=jax.ShapeDtypeStruct((t, n), out_dtype),
        scratch_shapes=[pltpu.VMEM((TM, D_MODEL), BF16)],
        compiler_params=_params(("parallel", "arbitrary")),
        name="inproj",
    )(x, mod, g.reshape(1, D_MODEL), w, col_scale)


def _q_kernel(cq_ref, g_ref, wa_ref, wb_ref, cos_ref, sin_ref, o_ref):
    xn = _rms(cq_ref[...], g_ref[...]).astype(BF16)
    ya = _dot(xn, wa_ref[...])
    yb = _dot(xn, wb_ref[...])
    cos, sin = cos_ref[...], sin_ref[...]
    for h in range(MLA_HEADS):
        lo = h * QK_PAD
        o_ref[:, lo:lo + LANES] = (ya[:, lo:lo + LANES] * MLA_SCALE).astype(BF16)
        rope = ya[:, lo + LANES:lo + QK_PAD] * cos + yb[:, h * LANES:(h + 1) * LANES] * sin
        o_ref[:, lo + LANES:lo + QK_PAD] = (rope * MLA_SCALE).astype(BF16)


def _q_heads(p1, g, wa, wb, cos_t, sin_t, table_block):
    t = p1.shape[0]
    return pl.pallas_call(
        _q_kernel,
        grid=(t // TM,),
        in_specs=[
            pl.BlockSpec((TM, Q_LORA), lambda i: (i, 0)),
            pl.BlockSpec((1, Q_LORA), lambda i: (0, 0)),
            pl.BlockSpec((Q_LORA, MLA_HEADS * QK_PAD), lambda i: (0, 0)),
            pl.BlockSpec((Q_LORA, MLA_HEADS * LANES), lambda i: (0, 0)),
            pl.BlockSpec((TM, LANES), lambda i: (table_block(i), 0)),
            pl.BlockSpec((TM, LANES), lambda i: (table_block(i), 0)),
        ],
        out_specs=pl.BlockSpec((TM, MLA_HEADS * QK_PAD), lambda i: (i, 0)),
        out_shape=jax.ShapeDtypeStruct((t, MLA_HEADS * QK_PAD), BF16),
        compiler_params=_params(("parallel",)),
        name="mla_q",
    )(p1, g.reshape(1, Q_LORA), wa, wb, cos_t, sin_t)


def _kv_kernel(ckv_ref, kra_ref, krb_ref, g_ref, wk_ref, wv_ref, cos_ref, sin_ref, k_ref, v_ref):
    xn = _rms(ckv_ref[...], g_ref[...]).astype(BF16)
    yk = _dot(xn, wk_ref[...])
    v_ref[...] = _dot(xn, wv_ref[...]).astype(BF16)
    kr = (kra_ref[...] * cos_ref[...] + krb_ref[...] * sin_ref[...]).astype(BF16)
    for h in range(MLA_HEADS):
        lo = h * QK_PAD
        k_ref[:, lo:lo + LANES] = yk[:, h * LANES:(h + 1) * LANES].astype(BF16)
        k_ref[:, lo + LANES:lo + QK_PAD] = kr


def _kv_heads(p1, g, wk, wv, cos_t, sin_t, table_block):
    t = p1.shape[0]
    kr_col = 2 * Q_LORA // LANES
    return pl.pallas_call(
        _kv_kernel,
        grid=(t // TM,),
        in_specs=[
            pl.BlockSpec((TM, KV_LORA), lambda i: (i, 1)),
            pl.BlockSpec((TM, LANES), lambda i: (i, kr_col)),
            pl.BlockSpec((TM, LANES), lambda i: (i, kr_col + 1)),
            pl.BlockSpec((1, KV_LORA), lambda i: (0, 0)),
            pl.BlockSpec((KV_LORA, MLA_HEADS * QK_NOPE), lambda i: (0, 0)),
            pl.BlockSpec((KV_LORA, D_MLA_OUT), lambda i: (0, 0)),
            pl.BlockSpec((TM, LANES), lambda i: (table_block(i), 0)),
            pl.BlockSpec((TM, LANES), lambda i: (table_block(i), 0)),
        ],
        out_specs=[
            pl.BlockSpec((TM, MLA_HEADS * QK_PAD), lambda i: (i, 0)),
            pl.BlockSpec((TM, D_MLA_OUT), lambda i: (i, 0)),
        ],
        out_shape=[
            jax.ShapeDtypeStruct((t, MLA_HEADS * QK_PAD), BF16),
            jax.ShapeDtypeStruct((t, D_MLA_OUT), BF16),
        ],
        compiler_params=_params(("parallel",)),
        name="mla_kv",
    )(p1, p1, p1, g.reshape(1, KV_LORA), wk, wv, cos_t, sin_t)


def _softmax_pv(s_parts, v_parts):
    m = functools.reduce(jnp.maximum, [jnp.max(s, axis=-1, keepdims=True) for s in s_parts])
    p_parts = [jnp.exp(s - m) for s in s_parts]
    l = functools.reduce(jnp.add, [jnp.sum(p, axis=-1, keepdims=True) for p in p_parts])
    o = functools.reduce(jnp.add, [_dot(p.astype(BF16), v) for p, v in zip(p_parts, v_parts)])
    return o * (1.0 / l)


def _mla_kernel(q_ref, kl_ref, kc_ref, vl_ref, vc_ref, o_ref):
    def scores(c):
        q = q_ref[c * MLA_QC:(c + 1) * MLA_QC, :]
        return [_dot_nt(q, kl_ref[...]), _dot_nt(q, kc_ref[...])]

    n_chunks = TQ // MLA_QC
    s = scores(0)
    for c in range(n_chunks):
        s_next = scores(c + 1) if c + 1 < n_chunks else None
        o_ref[c * MLA_QC:(c + 1) * MLA_QC, :] = _softmax_pv(s, [vl_ref[...], vc_ref[...]]).astype(o_ref.dtype)
        s = s_next


def _mla_latent(q, k_lat, k_ctx, v_lat, v_ctx):
    nq = SEQ // TQ
    return pl.pallas_call(
        _mla_kernel,
        grid=(BATCH, MLA_HEADS, nq),
        in_specs=[
            pl.BlockSpec((TQ, QK_PAD), lambda b, h, i: (b * nq + i, h)),
            pl.BlockSpec((SEQ, QK_PAD), lambda b, h, i: (b, h)),
            pl.BlockSpec((CTX_LEN, QK_PAD), lambda b, h, i: (b, h)),
            pl.BlockSpec((SEQ, V_HEAD), lambda b, h, i: (b, h)),
            pl.BlockSpec((CTX_LEN, V_HEAD), lambda b, h, i: (b, h)),
        ],
        out_specs=pl.BlockSpec((TQ, V_HEAD), lambda b, h, i: (b * nq + i, h)),
        out_shape=jax.ShapeDtypeStruct((BATCH * SEQ, D_MLA_OUT), BF16),
        compiler_params=_params(("parallel", "parallel", "arbitrary")),
        name="mla_latent",
    )(q, k_lat, k_ctx, v_lat, v_ctx)


def _ctx_attn_kernel(q_ref, k_ref, v_ref, o_ref):
    o_ref[...] = _softmax_pv([_dot_nt(q_ref[...], k_ref[...])], [v_ref[...]]).astype(o_ref.dtype)


def _ctx_attention(q, k, v, width, q_off, k_off, v_off):
    return pl.pallas_call(
        _ctx_attn_kernel,
        grid=(BATCH, MLA_HEADS),
        in_specs=[
            pl.BlockSpec((CTX_LEN, width), lambda b, h: (b, q_off + h)),
            pl.BlockSpec((CTX_LEN, width), lambda b, h: (b, k_off + h)),
            pl.BlockSpec((CTX_LEN, V_HEAD), lambda b, h: (b, v_off + h)),
        ],
        out_specs=pl.BlockSpec((CTX_LEN, V_HEAD), lambda b, h: (b, h)),
        out_shape=jax.ShapeDtypeStruct((BATCH * CTX_LEN, MLA_HEADS * V_HEAD), BF16),
        compiler_params=_params(("parallel", "parallel")),
        name="ctx_attention",
    )(q, k, v)


def _na_group_window(g):
    if g == 0:
        return 0, 0
    if g == 1:
        return 1, 0
    if g == N_NA_GROUPS - 1:
        return 3, GRID_H - NA_WIN_ROWS
    return 2, NA_GROUP_ROWS * g - WIN_H // 2


def _na_kernel(q_ref, k_ref, v_ref, kc_ref, vc_ref, tab_ref, o_ref):
    gq = NA_GROUP_ROWS * GRID_W
    gk = NA_WIN_ROWS * GRID_W

    def scores(g):
        var, ks = _na_group_window(g)
        q = q_ref[g * gq:(g + 1) * gq, :]
        kw = k_ref[ks * GRID_W:ks * GRID_W + gk, :]
        return [_dot_nt(q, kw) + tab_ref[var], _dot_nt(q, kc_ref[...])]

    s = scores(0)
    for g in range(N_NA_GROUPS):
        s_next = scores(g + 1) if g + 1 < N_NA_GROUPS else None
        _, ks = _na_group_window(g)
        vw = v_ref[ks * GRID_W:ks * GRID_W + gk, :]
        o_ref[g * gq:(g + 1) * gq, :] = _softmax_pv(s, [vw, vc_ref[...]]).astype(o_ref.dtype)
        s = s_next


def _na_tables(rpb):
    nh, nr, nc = rpb.shape
    v = jnp.pad(rpb.astype(F32), ((0, 0), (0, 0), (GRID_W - WIN_W, GRID_W + WIN_W - nc)))
    flat = jnp.broadcast_to(v[:, :, None, :], (nh, nr, GRID_W, 2 * GRID_W)).reshape(nh, nr, 2 * GRID_W * GRID_W)
    toe = flat[:, :, :GRID_W * (2 * GRID_W - 1)].reshape(nh, nr, GRID_W, 2 * GRID_W - 1)[..., GRID_W - 1:]
    row_pad = WIN_H // 2
    toe = jnp.pad(toe, ((0, 0), (row_pad, row_pad), (0, 0), (0, 0)))

    qa = np.arange(NA_GROUP_ROWS).reshape(-1, 1, 1, 1)
    qc = np.arange(GRID_W).reshape(1, -1, 1, 1)
    ki = np.arange(NA_WIN_ROWS).reshape(1, 1, -1, 1)
    kc = np.arange(GRID_W).reshape(1, 1, 1, -1)
    col_start = np.clip(qc - WIN_W // 2, 0, GRID_W - WIN_W)
    col_ok = (kc >= col_start) & (kc < col_start + WIN_W)
    tables = []
    for g in (0, 1, 2, N_NA_GROUPS - 1):
        _, ks = _na_group_window(g)
        r = NA_GROUP_ROWS * g + qa
        rs = np.clip(r - WIN_H // 2, 0, GRID_H - WIN_H)
        krow = ks + ki
        ok = np.broadcast_to((krow >= rs) & (krow < rs + WIN_H) & col_ok,
                             (NA_GROUP_ROWS, GRID_W, NA_WIN_ROWS, GRID_W))
        per_row = []
        for a in range(NA_GROUP_ROWS):
            base = ks - (NA_GROUP_ROWS * g + a) + WIN_H - 1 + row_pad
            assert 0 <= base and base + NA_WIN_ROWS <= nr + 2 * row_pad
            per_row.append(toe[:, base:base + NA_WIN_ROWS])
        bias = jnp.stack(per_row, axis=1).transpose(0, 1, 3, 2, 4)
        bias = jnp.where(ok[None], bias, NEG)
        tables.append(bias.reshape(nh, NA_GROUP_ROWS * GRID_W, NA_WIN_ROWS * GRID_W))
    return jnp.stack(tables)


def _na_latent(p2_lat, p2_ctx, tables):
    nh = NA_HEADS
    gq = NA_GROUP_ROWS * GRID_W
    gk = NA_WIN_ROWS * GRID_W
    return pl.pallas_call(
        _na_kernel,
        grid=(nh, BATCH),
        in_specs=[
            pl.BlockSpec((SEQ, NA_HEAD_DIM), lambda h, b: (b, h)),
            pl.BlockSpec((SEQ, NA_HEAD_DIM), lambda h, b: (b, nh + h)),
            pl.BlockSpec((SEQ, NA_HEAD_DIM), lambda h, b: (b, 2 * nh + h)),
            pl.BlockSpec((CTX_LEN, NA_HEAD_DIM), lambda h, b: (b, nh + h)),
            pl.BlockSpec((CTX_LEN, NA_HEAD_DIM), lambda h, b: (b, 2 * nh + h)),
            pl.BlockSpec((4, None, gq, gk), lambda h, b: (0, h, 0, 0)),
        ],
        out_specs=pl.BlockSpec((SEQ, NA_HEAD_DIM), lambda h, b: (b, h)),
        out_shape=jax.ShapeDtypeStruct((BATCH * SEQ, D_NA), BF16),
        compiler_params=_params(("parallel", "parallel")),
        name="na_latent",
    )(p2_lat, p2_lat, p2_lat, p2_ctx, p2_ctx, tables)


def _merge_kernel(a_ref, b_ref, h_ref, mod_ref, ga_ref, gb_ref, wa_ref, wb_ref, o_ref, *, k0):
    an = _rms(a_ref[...].astype(F32), ga_ref[...]).astype(BF16)
    bn = _rms(b_ref[...].astype(F32), gb_ref[...]).astype(BF16)
    y = _dot(an, wa_ref[...]) + _dot(bn, wb_ref[...])
    o_ref[...] = h_ref[...] + mod_ref[0, k0:k0 + 1, :] * y


def _merge(a, b, h, mod, row_of_tile, out_norm, w_out, k0):
    t = h.shape[0]
    g2 = out_norm.reshape(2, 1, D_MLA_OUT)
    return pl.pallas_call(
        functools.partial(_merge_kernel, k0=k0),
        grid=(t // TM,),
        in_specs=[
            pl.BlockSpec((TM, D_MLA_OUT), lambda i: (i, 0)),
            pl.BlockSpec((TM, D_NA), lambda i: (i, 0)),
            pl.BlockSpec((TM, D_MODEL), lambda i: (i, 0)),
            _mod_spec(row_of_tile, TM),
            pl.BlockSpec((None, 1, D_MLA_OUT), lambda i: (0, 0, 0)),
            pl.BlockSpec((None, 1, D_NA), lambda i: (1, 0, 0)),
            pl.BlockSpec((D_MLA_OUT, D_MODEL), lambda i: (0, 0)),
            pl.BlockSpec((D_NA, D_MODEL), lambda i: (1, 0)),
        ],
        out_specs=pl.BlockSpec((TM, D_MODEL), lambda i: (i, 0)),
        out_shape=jax.ShapeDtypeStruct((t, D_MODEL), F32),
        compiler_params=_params(("parallel",)),
        name="merge",
    )(a, b, h, mod, g2, g2, w_out, w_out)


def _rope_tables():
    t = np.arange(SEQ)
    pos = np.stack([t // GRID_W, t % GRID_W], axis=-1).astype(np.float32)
    inv_freq = jnp.asarray(ROPE_THETA, F32) ** (-jnp.arange(N_FREQ, dtype=F32) / N_FREQ)
    ang = jnp.asarray(pos)[:, :, None] * inv_freq
    cos, sin = jnp.cos(ang), jnp.sin(ang)
    cos_t = jnp.concatenate([cos, cos], axis=-1).reshape(SEQ, QK_ROPE)
    sin_t = jnp.concatenate([-sin, sin], axis=-1).reshape(SEQ, QK_ROPE)
    pad = jnp.zeros((SEQ, LANES - QK_ROPE), F32)
    cos_t = jnp.concatenate([cos_t, pad], axis=-1)
    sin_t = jnp.concatenate([sin_t, pad], axis=-1)
    ident = jnp.concatenate([jnp.ones((TM, QK_ROPE), F32), jnp.zeros((TM, LANES - QK_ROPE), F32)], axis=-1)
    return (jnp.concatenate([cos_t, ident], axis=0),
            jnp.concatenate([sin_t, jnp.zeros((TM, LANES), F32)], axis=0))


def _rotate_half_cols(w):
    blocks = w.reshape(w.shape[:-1] + (2, 2, N_FREQ))
    return blocks[..., ::-1, :].reshape(w.shape)


def _ff_tiles(w):
    return w.astype(BF16).reshape(D_MODEL, D_FF // TF, TF).transpose(1, 0, 2)


def _layer_weights(w_in, w_uq, w_ukv):
    kr = w_in[:, 2 * Q_LORA:2 * Q_LORA + QK_ROPE]
    zpad = jnp.zeros((D_MODEL, LANES - QK_ROPE), F32)
    w_p1 = jnp.concatenate([w_in[:, :2 * Q_LORA], kr, zpad, _rotate_half_cols(kr), zpad], axis=1).astype(BF16)
    w_p2 = w_in[:, 2 * Q_LORA + QK_ROPE:].astype(BF16)
    uq = w_uq.reshape(Q_LORA, MLA_HEADS, QK_NOPE + QK_ROPE)
    hz = jnp.zeros((Q_LORA, MLA_HEADS, QK_PAD - QK_NOPE - QK_ROPE), F32)
    wa = jnp.concatenate([uq, hz], axis=-1).reshape(Q_LORA, MLA_HEADS * QK_PAD).astype(BF16)
    wb = jnp.concatenate([_rotate_half_cols(uq[..., QK_NOPE:]), hz], axis=-1)
    wb = wb.reshape(Q_LORA, MLA_HEADS * LANES).astype(BF16)
    ukv = w_ukv.reshape(KV_LORA, MLA_HEADS, QK_NOPE + V_HEAD)
    wk = ukv[..., :QK_NOPE].reshape(KV_LORA, MLA_HEADS * QK_NOPE).astype(BF16)
    wv = ukv[..., QK_NOPE:].reshape(KV_LORA, D_MLA_OUT).astype(BF16)
    return w_p1, w_p2, wa, wb, wk, wv


def kernel(x, c, ctx, c_ctx, ada_w, ada_b, ffn1_norm, ffn1_w1, ffn1_w3, ffn1_w2, mix_norm, w_in, qa_norm, w_uq,
           kva_norm, w_ukv, na_rpb, out_norm, w_out, ffn2_norm, ffn2_w1, ffn2_w3, ffn2_w2, final_norm):
    c_rows = jnp.concatenate([c, c_ctx[None, :], jnp.zeros((MOD_ROWS - BATCH - 1, D_MODEL), F32)], axis=0)
    mod = _modulation(c_rows, ada_w, ada_b)
    cos_t, sin_t = _rope_tables()
    ones_p1 = jnp.ones((1, P1_COLS), F32)
    p2_scale = jnp.concatenate([jnp.full((1, D_NA), NA_SCALE, F32), jnp.ones((1, 2 * D_NA), F32)], axis=1)
    tiles_per_sample = SEQ // TM

    h = x.reshape(BATCH * SEQ, D_MODEL)
    hc = ctx.reshape(BATCH * CTX_LEN, D_MODEL)
    for l in range(DEPTH):
        last = l == DEPTH - 1
        lat_row = lambda i, tm, l=l: l * MOD_ROWS + (i * tm) // SEQ
        ctx_row = lambda i, tm, l=l: l * MOD_ROWS + CTX_MOD_ROW
        lat_tab = lambda i: i % tiles_per_sample
        ctx_tab = lambda i: tiles_per_sample
        w1, w3, w2 = _ff_tiles(ffn1_w1[l]), _ff_tiles(ffn1_w3[l]), ffn1_w2[l].astype(BF16)
        h = _ffn(h, mod, lat_row, ffn1_norm[l], w1, w3, w2, 0)
        hc = _ffn(hc, mod, ctx_row, ffn1_norm[l], w1, w3, w2, 0)

        w_p1, w_p2, wa, wb, wk, wv = _layer_weights(w_in[l], w_uq[l], w_ukv[l])
        tables = _na_tables(na_rpb[l])
        streams = []
        for hs, row, tab in ((h, lat_row, lat_tab), (hc, ctx_row, ctx_tab)):
            p1 = _inproj(hs, mod, row, mix_norm[l], w_p1, ones_p1, P1_COLS, F32, 3)
            p2 = _inproj(hs, mod, row, mix_norm[l], w_p2, p2_scale, 1024, BF16, 3)
            k, v = _kv_heads(p1, kva_norm[l], wk, wv, cos_t, sin_t, tab)
            streams.append((p1, p2, k, v, row, tab))
        (p1, p2, k, v, _, _), (p1c, p2c, kc, vc, _, _) = streams

        q = _q_heads(p1, qa_norm[l], wa, wb, cos_t, sin_t, lat_tab)
        a_lat = _mla_latent(q, k, kc, v, vc)
        b_lat = _na_latent(p2, p2c, tables)
        h = _merge(a_lat, b_lat, h, mod, lat_row, out_norm[l], w_out[l].astype(BF16), 5)
        w1, w3, w2 = _ff_tiles(ffn2_w1[l]), _ff_tiles(ffn2_w3[l]), ffn2_w2[l].astype(BF16)
        h = _ffn(h, mod, lat_row, ffn2_norm[l], w1, w3, w2, 6, final_g=final_norm if last else None)
        if not last:
            qc = _q_heads(p1c, qa_norm[l], wa, wb, cos_t, sin_t, ctx_tab)
            a_ctx = _ctx_attention(qc, kc, vc, QK_PAD, 0, 0, 0)
            b_ctx = _ctx_attention(p2c, p2c, p2c, NA_HEAD_DIM, 0, NA_HEADS, 2 * NA_HEADS)
            hc = _merge(a_ctx, b_ctx, hc, mod, ctx_row, out_norm[l], w_out[l].astype(BF16), 5)
            hc = _ffn(hc, mod, ctx_row, ffn2_norm[l], w1, w3, w2, 6)
    return h.reshape(BATCH, SEQ, D_MODEL)
```

```python
import functools

import numpy as np
import jax
import jax.numpy as jnp
from jax import lax
from jax.experimental import pallas as pl
from jax.experimental.pallas import tpu as pltpu

D_MODEL = 2048
BATCH = 16
SEQ = 2048
DEPTH = 2
GRID_W = 64
GRID_H = SEQ // GRID_W
CTX_LEN = 256
MLA_HEADS = 8
QK_NOPE = 128
QK_ROPE = 64
V_HEAD = 128
Q_LORA = 512
KV_LORA = 512
N_FREQ = QK_ROPE // 4
ROPE_THETA = 10000.0
MLA_SCALE = (QK_NOPE + QK_ROPE) ** -0.5
NA_HEADS = 8
NA_HEAD_DIM = 128
WIN_H = 8
WIN_W = 16
NA_SCALE = NA_HEAD_DIM ** -0.5
D_MLA_OUT = MLA_HEADS * V_HEAD
D_NA = NA_HEADS * NA_HEAD_DIM
D_FF = 5632
N_MOD = 9
EPS = 1e-6
NEG = -1e30

F32 = jnp.float32
BF16 = jnp.bfloat16

LANES = 128
V7X_VMEM_LIMIT = 56 * 1024 * 1024

MOD_ROWS = 24
CTX_MOD_ROW = BATCH
QK_PAD = 256
P1_COLS = 2 * Q_LORA + 2 * LANES
NA_GROUP_ROWS = 4
NA_WIN_ROWS = 12
N_NA_GROUPS = GRID_H // NA_GROUP_ROWS

ROW_CHUNK = 16
MIX_TM = 256
TM = 512
FFN_TM = 1024
FFN_MC = 512
FFN_NC = 512
TF = 512
TQ = 1024
MLA_QC = 256


def _params(sem, vmem=V7X_VMEM_LIMIT):
    return pltpu.CompilerParams(dimension_semantics=sem, vmem_limit_bytes=vmem)


def _rms(x, g):
    return x * lax.rsqrt(jnp.mean(x * x, axis=-1, keepdims=True) + EPS) * g


def _silu(a):
    return a * (1.0 / (1.0 + jnp.exp(-a)))


def _dot(a, b):
    return jnp.dot(a, b, preferred_element_type=F32)


def _dot_nt(a, b):
    return lax.dot_general(a, b, (((1,), (1,)), ((), ())), preferred_element_type=F32)


def _mod_kernel(s_ref, w_ref, b_ref, o_ref):
    s = _silu(s_ref[...])
    o_ref[...] = _dot(s.astype(BF16), w_ref[...].astype(BF16)) + b_ref[...]


def _modulation(c_rows, ada_w, ada_b):
    tn = 1024
    n = N_MOD * D_MODEL
    out = pl.pallas_call(
        _mod_kernel,
        grid=(DEPTH, n // tn),
        in_specs=[
            pl.BlockSpec((MOD_ROWS, D_MODEL), lambda l, j: (0, 0)),
            pl.BlockSpec((None, D_MODEL, tn), lambda l, j: (l, 0, j)),
            pl.BlockSpec((None, 1, tn), lambda l, j: (l, 0, j)),
        ],
        out_specs=pl.BlockSpec((None, MOD_ROWS, tn), lambda l, j: (l, 0, j)),
        out_shape=jax.ShapeDtypeStruct((DEPTH, MOD_ROWS, n), F32),
        compiler_params=_params(("parallel", "parallel")),
        name="adaln_mod",
    )(c_rows, ada_w, ada_b.reshape(DEPTH, 1, n))
    return out.reshape(DEPTH * MOD_ROWS, N_MOD, D_MODEL)


def _norm_modulate_rows(x_ref, g_ref, mod_ref, k0, xn_ref):
    gain = g_ref[...] * (1.0 + mod_ref[0, k0 + 1:k0 + 2, :])
    shift = mod_ref[0, k0:k0 + 1, :]

    def body(r, carry):
        rows = pl.ds(pl.multiple_of(r * ROW_CHUNK, ROW_CHUNK), ROW_CHUNK)
        x = x_ref[rows, :]
        y = x * lax.rsqrt(jnp.mean(x * x, axis=-1, keepdims=True) + EPS)
        xn_ref[rows, :] = (y * gain + shift).astype(BF16)
        return carry

    lax.fori_loop(0, x_ref.shape[0] // ROW_CHUNK, body, 0, unroll=8)


def _mod_spec(row_of_tile, tm):
    return pl.BlockSpec((1, N_MOD, D_MODEL), lambda i, *_: (row_of_tile(i, tm), 0, 0))


def _ffn_kernel(x_ref, mod_ref, g_ref, w1_ref, w3_ref, w2_ref, gf_ref, o_ref, xn_ref, *, k0, final):
    j = pl.program_id(1)

    @pl.when(j == 0)
    def _():
        _norm_modulate_rows(x_ref, g_ref, mod_ref, k0, xn_ref)
        o_ref[...] = jnp.zeros_like(o_ref)

    for m in range(FFN_TM // FFN_MC):
        rows = slice(m * FFN_MC, (m + 1) * FFN_MC)
        xn = xn_ref[rows, :]
        a = _dot(xn, w1_ref[...])
        b = _dot(xn, w3_ref[...])
        gated = (_silu(a) * b).astype(BF16)
        for n in range(D_MODEL // FFN_NC):
            cols = slice(n * FFN_NC, (n + 1) * FFN_NC)
            o_ref[rows, cols] += _dot(gated, w2_ref[:, cols])

    @pl.when(j == pl.num_programs(1) - 1)
    def _():
        h = x_ref[...] + (0.5 * mod_ref[0, k0 + 2:k0 + 3, :]) * o_ref[...]
        o_ref[...] = _rms(h, gf_ref[...]) if final else h


def _ffn(x, mod, row_of_tile, g, w1, w3, w2, k0, final_g=None):
    t = x.shape[0]
    final = final_g is not None
    gf = final_g if final else g
    return pl.pallas_call(
        functools.partial(_ffn_kernel, k0=k0, final=final),
        grid=(t // FFN_TM, D_FF // TF),
        in_specs=[
            pl.BlockSpec((FFN_TM, D_MODEL), lambda i, j: (i, 0), pipeline_mode=pl.Buffered(1)),
            _mod_spec(row_of_tile, FFN_TM),
            pl.BlockSpec((1, D_MODEL), lambda i, j: (0, 0)),
            pl.BlockSpec((D_MODEL, TF), lambda i, j: (0, j)),
            pl.BlockSpec((D_MODEL, TF), lambda i, j: (0, j)),
            pl.BlockSpec((TF, D_MODEL), lambda i, j: (j, 0)),
            pl.BlockSpec((1, D_MODEL), lambda i, j: (0, 0)),
        ],
        out_specs=pl.BlockSpec((FFN_TM, D_MODEL), lambda i, j: (i, 0)),
        out_shape=jax.ShapeDtypeStruct((t, D_MODEL), F32),
        scratch_shapes=[pltpu.VMEM((FFN_TM, D_MODEL), BF16)],
        compiler_params=_params(("parallel", "arbitrary")),
        name="ffn",
    )(x, mod, g.reshape(1, D_MODEL), w1, w3, w2, gf.reshape(1, D_MODEL))


def _mixer_in_kernel(x_ref, mod_ref, g_ref, wp1_ref, wp2_ref, gq_ref, gkv_ref, wa_ref, wb_ref, wk_ref, wv_ref,
                     cos_ref, sin_ref, q_ref, k_ref, v_ref, p2_ref, xn_ref, *, k0):
    _norm_modulate_rows(x_ref, g_ref, mod_ref, k0, xn_ref)
    xn = xn_ref[...]

    for n in range(3):
        cols = slice(n * D_NA, (n + 1) * D_NA)
        y2 = _dot(xn, wp2_ref[:, cols])
        p2_ref[:, cols] = (y2 * NA_SCALE if n == 0 else y2).astype(BF16)

    p1 = _dot(xn, wp1_ref[...])
    cos, sin = cos_ref[...], sin_ref[...]

    cqn = _rms(p1[:, :Q_LORA], gq_ref[...]).astype(BF16)
    ya = _dot(cqn, wa_ref[...])
    yb = _dot(cqn, wb_ref[...])
    for h in range(MLA_HEADS):
        lo = h * QK_PAD
        q_ref[:, lo:lo + LANES] = (ya[:, lo:lo + LANES] * MLA_SCALE).astype(BF16)
        rope = ya[:, lo + LANES:lo + QK_PAD] * cos + yb[:, h * LANES:(h + 1) * LANES] * sin
        q_ref[:, lo + LANES:lo + QK_PAD] = (rope * MLA_SCALE).astype(BF16)

    ckvn = _rms(p1[:, Q_LORA:Q_LORA + KV_LORA], gkv_ref[...]).astype(BF16)
    yk = _dot(ckvn, wk_ref[...])
    v_ref[...] = _dot(ckvn, wv_ref[...]).astype(BF16)
    kr_lo = Q_LORA + KV_LORA
    kr = (p1[:, kr_lo:kr_lo + LANES] * cos + p1[:, kr_lo + LANES:kr_lo + 2 * LANES] * sin).astype(BF16)
    for h in range(MLA_HEADS):
        lo = h * QK_PAD
        k_ref[:, lo:lo + LANES] = yk[:, h * LANES:(h + 1) * LANES].astype(BF16)
        k_ref[:, lo + LANES:lo + QK_PAD] = kr


def _mixer_in(x, mod, row_of_tile, table_block, g, w_p1, w_p2, gq, gkv, wa, wb, wk, wv, cos_t, sin_t, k0):
    t = x.shape[0]
    tm = MIX_TM

    def resident(shape):
        return pl.BlockSpec(shape, lambda i: (0,) * len(shape), pipeline_mode=pl.Buffered(1))

    return pl.pallas_call(
        functools.partial(_mixer_in_kernel, k0=k0),
        grid=(t // tm,),
        in_specs=[
            pl.BlockSpec((tm, D_MODEL), lambda i: (i, 0)),
            _mod_spec(row_of_tile, tm),
            resident((1, D_MODEL)),
            resident((D_MODEL, P1_COLS)),
            resident((D_MODEL, 3 * D_NA)),
            resident((1, Q_LORA)),
            resident((1, KV_LORA)),
            resident((Q_LORA, MLA_HEADS * QK_PAD)),
            resident((Q_LORA, MLA_HEADS * LANES)),
            resident((KV_LORA, MLA_HEADS * QK_NOPE)),
            resident((KV_LORA, D_MLA_OUT)),
            pl.BlockSpec((tm, LANES), lambda i: (table_block(i, tm), 0)),
            pl.BlockSpec((tm, LANES), lambda i: (table_block(i, tm), 0)),
        ],
        out_specs=[
            pl.BlockSpec((tm, MLA_HEADS * QK_PAD), lambda i: (i, 0)),
            pl.BlockSpec((tm, MLA_HEADS * QK_PAD), lambda i: (i, 0)),
            pl.BlockSpec((tm, D_MLA_OUT), lambda i: (i, 0)),
            pl.BlockSpec((tm, 3 * D_NA), lambda i: (i, 0)),
        ],
        out_shape=[
            jax.ShapeDtypeStruct((t, MLA_HEADS * QK_PAD), BF16),
            jax.ShapeDtypeStruct((t, MLA_HEADS * QK_PAD), BF16),
            jax.ShapeDtypeStruct((t, D_MLA_OUT), BF16),
            jax.ShapeDtypeStruct((t, 3 * D_NA), BF16),
        ],
        scratch_shapes=[pltpu.VMEM((tm, D_MODEL), BF16)],
        compiler_params=_params(("parallel",)),
        name="mixer_in",
    )(x, mod, g.reshape(1, D_MODEL), w_p1, w_p2, gq.reshape(1, Q_LORA), gkv.reshape(1, KV_LORA), wa, wb, wk, wv,
      cos_t, sin_t)


def _softmax_pv(s_parts, v_parts):
    m = functools.reduce(jnp.maximum, [jnp.max(s, axis=-1, keepdims=True) for s in s_parts])
    p_parts = [jnp.exp(s - m) for s in s_parts]
    l = functools.reduce(jnp.add, [jnp.sum(p, axis=-1, keepdims=True) for p in p_parts])
    o = functools.reduce(jnp.add, [_dot(p.astype(BF16), v) for p, v in zip(p_parts, v_parts)])
    return o * (1.0 / l)


def _mla_kernel(q_ref, kl_ref, kc_ref, vl_ref, vc_ref, o_ref):
    def scores(c):
        q = q_ref[c * MLA_QC:(c + 1) * MLA_QC, :]
        return [_dot_nt(q, kl_ref[...]), _dot_nt(q, kc_ref[...])]

    n_chunks = TQ // MLA_QC
    s = scores(0)
    for c in range(n_chunks):
        s_next = scores(c + 1) if c + 1 < n_chunks else None
        o_ref[c * MLA_QC:(c + 1) * MLA_QC, :] = _softmax_pv(s, [vl_ref[...], vc_ref[...]]).astype(o_ref.dtype)
        s = s_next


def _mla_latent(q, k_lat, k_ctx, v_lat, v_ctx):
    nq = SEQ // TQ
    return pl.pallas_call(
        _mla_kernel,
        grid=(BATCH, MLA_HEADS, nq),
        in_specs=[
            pl.BlockSpec((TQ, QK_PAD), lambda b, h, i: (b * nq + i, h)),
            pl.BlockSpec((SEQ, QK_PAD), lambda b, h, i: (b, h)),
            pl.BlockSpec((CTX_LEN, QK_PAD), lambda b, h, i: (b, h)),
            pl.BlockSpec((SEQ, V_HEAD), lambda b, h, i: (b, h)),
            pl.BlockSpec((CTX_LEN, V_HEAD), lambda b, h, i: (b, h)),
        ],
        out_specs=pl.BlockSpec((TQ, V_HEAD), lambda b, h, i: (b * nq + i, h)),
        out_shape=jax.ShapeDtypeStruct((BATCH * SEQ, D_MLA_OUT), BF16),
        compiler_params=_params(("parallel", "parallel", "arbitrary")),
        name="mla_latent",
    )(q, k_lat, k_ctx, v_lat, v_ctx)


def _ctx_attn_kernel(q_ref, k_ref, v_ref, o_ref):
    o_ref[...] = _softmax_pv([_dot_nt(q_ref[...], k_ref[...])], [v_ref[...]]).astype(o_ref.dtype)


def _ctx_attention(q, k, v, width, q_off, k_off, v_off):
    return pl.pallas_call(
        _ctx_attn_kernel,
        grid=(BATCH, MLA_HEADS),
        in_specs=[
            pl.BlockSpec((CTX_LEN, width), lambda b, h: (b, q_off + h)),
            pl.BlockSpec((CTX_LEN, width), lambda b, h: (b, k_off + h)),
            pl.BlockSpec((CTX_LEN, V_HEAD), lambda b, h: (b, v_off + h)),
        ],
        out_specs=pl.BlockSpec((CTX_LEN, V_HEAD), lambda b, h: (b, h)),
        out_shape=jax.ShapeDtypeStruct((BATCH * CTX_LEN, MLA_HEADS * V_HEAD), BF16),
        compiler_params=_params(("parallel", "parallel")),
        name="ctx_attention",
    )(q, k, v)


def _na_group_window(g):
    if g == 0:
        return 0, 0
    if g == 1:
        return 1, 0
    if g == N_NA_GROUPS - 1:
        return 3, GRID_H - NA_WIN_ROWS
    return 2, NA_GROUP_ROWS * g - WIN_H // 2


def _na_kernel(q_ref, k_ref, v_ref, kc_ref, vc_ref, tab_ref, o_ref):
    gq = NA_GROUP_ROWS * GRID_W
    gk = NA_WIN_ROWS * GRID_W

    def scores(g):
        var, ks = _na_group_window(g)
        q = q_ref[g * gq:(g + 1) * gq, :]
        kw = k_ref[ks * GRID_W:ks * GRID_W + gk, :]
        return [_dot_nt(q, kw) + tab_ref[var], _dot_nt(q, kc_ref[...])]

    s = scores(0)
    for g in range(N_NA_GROUPS):
        s_next = scores(g + 1) if g + 1 < N_NA_GROUPS else None
        _, ks = _na_group_window(g)
        vw = v_ref[ks * GRID_W:ks * GRID_W + gk, :]
        o_ref[g * gq:(g + 1) * gq, :] = _softmax_pv(s, [vw, vc_ref[...]]).astype(o_ref.dtype)
        s = s_next


def _na_tables(rpb):
    nh, nr, nc = rpb.shape
    v = jnp.pad(rpb.astype(F32), ((0, 0), (0, 0), (GRID_W - WIN_W, GRID_W + WIN_W - nc)))
    flat = jnp.broadcast_to(v[:, :, None, :], (nh, nr, GRID_W, 2 * GRID_W)).reshape(nh, nr, 2 * GRID_W * GRID_W)
    toe = flat[:, :, :GRID_W * (2 * GRID_W - 1)].reshape(nh, nr, GRID_W, 2 * GRID_W - 1)[..., GRID_W - 1:]
    row_pad = WIN_H // 2
    toe = jnp.pad(toe, ((0, 0), (row_pad, row_pad), (0, 0), (0, 0)))

    qa = np.arange(NA_GROUP_ROWS).reshape(-1, 1, 1, 1)
    qc = np.arange(GRID_W).reshape(1, -1, 1, 1)
    ki = np.arange(NA_WIN_ROWS).reshape(1, 1, -1, 1)
    kc = np.arange(GRID_W).reshape(1, 1, 1, -1)
    col_start = np.clip(qc - WIN_W // 2, 0, GRID_W - WIN_W)
    col_ok = (kc >= col_start) & (kc < col_start + WIN_W)
    tables = []
    for g in (0, 1, 2, N_NA_GROUPS - 1):
        _, ks = _na_group_window(g)
        r = NA_GROUP_ROWS * g + qa
        rs = np.clip(r - WIN_H // 2, 0, GRID_H - WIN_H)
        krow = ks + ki
        ok = np.broadcast_to((krow >= rs) & (krow < rs + WIN_H) & col_ok,
                             (NA_GROUP_ROWS, GRID_W, NA_WIN_ROWS, GRID_W))
        per_row = []
        for a in range(NA_GROUP_ROWS):
            base = ks - (NA_GROUP_ROWS * g + a) + WIN_H - 1 + row_pad
            assert 0 <= base and base + NA_WIN_ROWS <= nr + 2 * row_pad
            per_row.append(toe[:, base:base + NA_WIN_ROWS])
        bias = jnp.stack(per_row, axis=1).transpose(0, 1, 3, 2, 4)
        bias = jnp.where(ok[None], bias, NEG)
        tables.append(bias.reshape(nh, NA_GROUP_ROWS * GRID_W, NA_WIN_ROWS * GRID_W))
    return jnp.stack(tables)


def _na_latent(p2_lat, p2_ctx, tables):
    nh = NA_HEADS
    gq = NA_GROUP_ROWS * GRID_W
    gk = NA_WIN_ROWS * GRID_W
    return pl.pallas_call(
        _na_kernel,
        grid=(nh, BATCH),
        in_specs=[
            pl.BlockSpec((SEQ, NA_HEAD_DIM), lambda h, b: (b, h)),
            pl.BlockSpec((SEQ, NA_HEAD_DIM), lambda h, b: (b, nh + h)),
            pl.BlockSpec((SEQ, NA_HEAD_DIM), lambda h, b: (b, 2 * nh + h)),
            pl.BlockSpec((CTX_LEN, NA_HEAD_DIM), lambda h, b: (b, nh + h)),
            pl.BlockSpec((CTX_LEN, NA_HEAD_DIM), lambda h, b: (b, 2 * nh + h)),
            pl.BlockSpec((4, None, gq, gk), lambda h, b: (0, h, 0, 0)),
        ],
        out_specs=pl.BlockSpec((SEQ, NA_HEAD_DIM), lambda h, b: (b, h)),
        out_shape=jax.ShapeDtypeStruct((BATCH * SEQ, D_NA), BF16),
        compiler_params=_params(("parallel", "parallel")),
        name="na_latent",
    )(p2_lat, p2_lat, p2_lat, p2_ctx, p2_ctx, tables)


def _merge_kernel(a_ref, b_ref, h_ref, mod_ref, ga_ref, gb_ref, wa_ref, wb_ref, o_ref, *, k0):
    an = _rms(a_ref[...].astype(F32), ga_ref[...]).astype(BF16)
    bn = _rms(b_ref[...].astype(F32), gb_ref[...]).astype(BF16)
    y = _dot(an, wa_ref[...]) + _dot(bn, wb_ref[...])
    o_ref[...] = h_ref[...] + mod_ref[0, k0:k0 + 1, :] * y


def _merge(a, b, h, mod, row_of_tile, out_norm, w_out, k0):
    t = h.shape[0]
    g2 = out_norm.reshape(2, 1, D_MLA_OUT)
    return pl.pallas_call(
        functools.partial(_merge_kernel, k0=k0),
        grid=(t // TM,),
        in_specs=[
            pl.BlockSpec((TM, D_MLA_OUT), lambda i: (i, 0)),
            pl.BlockSpec((TM, D_NA), lambda i: (i, 0)),
            pl.BlockSpec((TM, D_MODEL), lambda i: (i, 0)),
            _mod_spec(row_of_tile, TM),
            pl.BlockSpec((None, 1, D_MLA_OUT), lambda i: (0, 0, 0)),
            pl.BlockSpec((None, 1, D_NA), lambda i: (1, 0, 0)),
            pl.BlockSpec((D_MLA_OUT, D_MODEL), lambda i: (0, 0)),
            pl.BlockSpec((D_NA, D_MODEL), lambda i: (1, 0)),
        ],
        out_specs=pl.BlockSpec((TM, D_MODEL), lambda i: (i, 0)),
        out_shape=jax.ShapeDtypeStruct((t, D_MODEL), F32),
        compiler_params=_params(("parallel",)),
        name="merge",
    )(a, b, h, mod, g2, g2, w_out, w_out)


def _rope_tables():
    t = np.arange(SEQ)
    pos = np.stack([t // GRID_W, t % GRID_W], axis=-1).astype(np.float32)
    inv_freq = jnp.asarray(ROPE_THETA, F32) ** (-jnp.arange(N_FREQ, dtype=F32) / N_FREQ)
    ang = jnp.asarray(pos)[:, :, None] * inv_freq
    cos, sin = jnp.cos(ang), jnp.sin(ang)
    cos_t = jnp.concatenate([cos, cos], axis=-1).reshape(SEQ, QK_ROPE)
    sin_t = jnp.concatenate([-sin, sin], axis=-1).reshape(SEQ, QK_ROPE)
    pad = jnp.zeros((SEQ, LANES - QK_ROPE), F32)
    cos_t = jnp.concatenate([cos_t, pad], axis=-1)
    sin_t = jnp.concatenate([sin_t, pad], axis=-1)
    ident = jnp.concatenate([jnp.ones((TM, QK_ROPE), F32), jnp.zeros((TM, LANES - QK_ROPE), F32)], axis=-1)
    return (jnp.concatenate([cos_t, ident], axis=0),
            jnp.concatenate([sin_t, jnp.zeros((TM, LANES), F32)], axis=0))


def _rotate_half_cols(w):
    blocks = w.reshape(w.shape[:-1] + (2, 2, N_FREQ))
    return blocks[..., ::-1, :].reshape(w.shape)


def _layer_weights(w_in, w_uq, w_ukv):
    kr = w_in[:, 2 * Q_LORA:2 * Q_LORA + QK_ROPE]
    zpad = jnp.zeros((D_MODEL, LANES - QK_ROPE), F32)
    w_p1 = jnp.concatenate([w_in[:, :2 * Q_LORA], kr, zpad, _rotate_half_cols(kr), zpad], axis=1).astype(BF16)
    w_p2 = w_in[:, 2 * Q_LORA + QK_ROPE:].astype(BF16)
    uq = w_uq.reshape(Q_LORA, MLA_HEADS, QK_NOPE + QK_ROPE)
    hz = jnp.zeros((Q_LORA, MLA_HEADS, QK_PAD - QK_NOPE - QK_ROPE), F32)
    wa = jnp.concatenate([uq, hz], axis=-1).reshape(Q_LORA, MLA_HEADS * QK_PAD).astype(BF16)
    wb = jnp.concatenate([_rotate_half_cols(uq[..., QK_NOPE:]), hz], axis=-1)
    wb = wb.reshape(Q_LORA, MLA_HEADS * LANES).astype(BF16)
    ukv = w_ukv.reshape(KV_LORA, MLA_HEADS, QK_NOPE + V_HEAD)
    wk = ukv[..., :QK_NOPE].reshape(KV_LORA, MLA_HEADS * QK_NOPE).astype(BF16)
    wv = ukv[..., QK_NOPE:].reshape(KV_LORA, D_MLA_OUT).astype(BF16)
    return w_p1, w_p2, wa, wb, wk, wv


def kernel(x, c, ctx, c_ctx, ada_w, ada_b, ffn1_norm, ffn1_w1, ffn1_w3, ffn1_w2, mix_norm, w_in, qa_norm, w_uq,
           kva_norm, w_ukv, na_rpb, out_norm, w_out, ffn2_norm, ffn2_w1, ffn2_w3, ffn2_w2, final_norm):
    c_rows = jnp.concatenate([c, c_ctx[None, :], jnp.zeros((MOD_ROWS - BATCH - 1, D_MODEL), F32)], axis=0)
    mod = _modulation(c_rows, ada_w, ada_b)
    cos_t, sin_t = _rope_tables()
    lat_tab = lambda i, tm: i % (SEQ // tm)
    ctx_tab = lambda i, tm: SEQ // tm

    h = x.reshape(BATCH * SEQ, D_MODEL)
    hc = ctx.reshape(BATCH * CTX_LEN, D_MODEL)
    for l in range(DEPTH):
        last = l == DEPTH - 1
        lat_row = lambda i, tm, l=l: l * MOD_ROWS + (i * tm) // SEQ
        ctx_row = lambda i, tm, l=l: l * MOD_ROWS + CTX_MOD_ROW
        w1, w3, w2 = ffn1_w1[l].astype(BF16), ffn1_w3[l].astype(BF16), ffn1_w2[l].astype(BF16)
        h = _ffn(h, mod, lat_row, ffn1_norm[l], w1, w3, w2, 0)
        hc = _ffn(hc, mod, ctx_row, ffn1_norm[l], w1, w3, w2, 0)

        w_p1, w_p2, wa, wb, wk, wv = _layer_weights(w_in[l], w_uq[l], w_ukv[l])
        tables = _na_tables(na_rpb[l])
        mix_w = (mix_norm[l], w_p1, w_p2, qa_norm[l], kva_norm[l], wa, wb, wk, wv, cos_t, sin_t, 3)
        q, k, v, p2 = _mixer_in(h, mod, lat_row, lat_tab, *mix_w)
        qc, kc, vc, p2c = _mixer_in(hc, mod, ctx_row, ctx_tab, *mix_w)
        a_lat = _mla_latent(q, k, kc, v, vc)
        b_lat = _na_latent(p2, p2c, tables)
        h = _merge(a_lat, b_lat, h, mod, lat_row, out_norm[l], w_out[l].astype(BF16), 5)
        w1, w3, w2 = ffn2_w1[l].astype(BF16), ffn2_w3[l].astype(BF16), ffn2_w2[l].astype(BF16)
        h = _ffn(h, mod, lat_row, ffn2_norm[l], w1, w3, w2, 6, final_g=final_norm if last else None)
        if not last:
            a_ctx = _ctx_attention(qc, kc, vc, QK_PAD, 0, 0, 0)
            b_ctx = _ctx_attention(p2c, p2c, p2c, NA_HEAD_DIM, 0, NA_HEADS, 2 * NA_HEADS)
            hc = _merge(a_ctx, b_ctx, hc, mod, ctx_row, out_norm[l], w_out[l].astype(BF16), 5)
            hc = _ffn(hc, mod, ctx_row, ffn2_norm[l], w1, w3, w2, 6)
    return h.reshape(BATCH, SEQ, D_MODEL)
```

```python
import functools

import numpy as np
import jax
import jax.numpy as jnp
from jax import lax
from jax.experimental import pallas as pl
from jax.experimental.pallas import tpu as pltpu

D_MODEL = 2048
BATCH = 16
SEQ = 2048
DEPTH = 2
GRID_W = 64
GRID_H = SEQ // GRID_W
CTX_LEN = 256
MLA_HEADS = 8
QK_NOPE = 128
QK_ROPE = 64
V_HEAD = 128
Q_LORA = 512
KV_LORA = 512
N_FREQ = QK_ROPE // 4
ROPE_THETA = 10000.0
MLA_SCALE = (QK_NOPE + QK_ROPE) ** -0.5
NA_HEADS = 8
NA_HEAD_DIM = 128
WIN_H = 8
WIN_W = 16
NA_SCALE = NA_HEAD_DIM ** -0.5
D_MLA_OUT = MLA_HEADS * V_HEAD
D_NA = NA_HEADS * NA_HEAD_DIM
D_FF = 5632
N_MOD = 9
EPS = 1e-6
NEG = -1e30

F32 = jnp.float32
BF16 = jnp.bfloat16

LANES = 128
V7X_VMEM_LIMIT = 60000 * 1024

MOD_ROWS = 24
CTX_MOD_ROW = BATCH
QK_PAD = 256
P1_COLS = 2 * Q_LORA + 2 * LANES
NA_GROUP_ROWS = 4
NA_WIN_ROWS = 12
N_NA_GROUPS = GRID_H // NA_GROUP_ROWS

ROW_CHUNK = 16
MIX_TM = 256
TM = 512
FFN_TM = 1024
FFN_MC = 512
FFN_NC = 512
TF = 512
TQ = 2048
MLA_QC = 512


def _params(sem, vmem=V7X_VMEM_LIMIT):
    return pltpu.CompilerParams(dimension_semantics=sem, vmem_limit_bytes=vmem)


def _rms(x, g):
    return x * lax.rsqrt(jnp.mean(x * x, axis=-1, keepdims=True) + EPS) * g


def _silu(a):
    return a * (1.0 / (1.0 + jnp.exp(-a)))


def _dot(a, b):
    return jnp.dot(a, b, preferred_element_type=F32)


def _dot_nt(a, b):
    return lax.dot_general(a, b, (((1,), (1,)), ((), ())), preferred_element_type=F32)


def _mod_kernel(s_ref, w_ref, b_ref, o_ref):
    s = _silu(s_ref[...])
    o_ref[...] = _dot(s.astype(BF16), w_ref[...].astype(BF16)) + b_ref[...]


def _modulation(c_rows, ada_w, ada_b):
    tn = 1024
    n = N_MOD * D_MODEL
    out = pl.pallas_call(
        _mod_kernel,
        grid=(DEPTH, n // tn),
        in_specs=[
            pl.BlockSpec((MOD_ROWS, D_MODEL), lambda l, j: (0, 0)),
            pl.BlockSpec((None, D_MODEL, tn), lambda l, j: (l, 0, j)),
            pl.BlockSpec((None, 1, tn), lambda l, j: (l, 0, j)),
        ],
        out_specs=pl.BlockSpec((None, MOD_ROWS, tn), lambda l, j: (l, 0, j)),
        out_shape=jax.ShapeDtypeStruct((DEPTH, MOD_ROWS, n), F32),
        compiler_params=_params(("parallel", "parallel")),
        name="adaln_mod",
    )(c_rows, ada_w, ada_b.reshape(DEPTH, 1, n))
    return out.reshape(DEPTH * MOD_ROWS, N_MOD, D_MODEL)


def _norm_modulate_rows(x_ref, g_ref, mod_ref, k0, xn_ref):
    gain = g_ref[...] * (1.0 + mod_ref[0, k0 + 1:k0 + 2, :])
    shift = mod_ref[0, k0:k0 + 1, :]

    def body(r, carry):
        rows = pl.ds(pl.multiple_of(r * ROW_CHUNK, ROW_CHUNK), ROW_CHUNK)
        x = x_ref[rows, :]
        y = x * lax.rsqrt(jnp.mean(x * x, axis=-1, keepdims=True) + EPS)
        xn_ref[rows, :] = (y * gain + shift).astype(BF16)
        return carry

    lax.fori_loop(0, x_ref.shape[0] // ROW_CHUNK, body, 0, unroll=8)


def _mod_spec(row_of_tile, tm):
    return pl.BlockSpec((1, N_MOD, D_MODEL), lambda i, *_: (row_of_tile(i, tm), 0, 0))


def _ffn_kernel(x_ref, mod_ref, g_ref, w1_ref, w3_ref, w2_ref, gf_ref, o_ref, xn_ref, *, k0, final):
    j = pl.program_id(1)

    @pl.when(j == 0)
    def _():
        _norm_modulate_rows(x_ref, g_ref, mod_ref, k0, xn_ref)
        o_ref[...] = jnp.zeros_like(o_ref)

    for m in range(FFN_TM // FFN_MC):
        rows = slice(m * FFN_MC, (m + 1) * FFN_MC)
        xn = xn_ref[rows, :]
        a = _dot(xn, w1_ref[...])
        b = _dot(xn, w3_ref[...])
        gated = (_silu(a) * b).astype(BF16)
        for n in range(D_MODEL // FFN_NC):
            cols = slice(n * FFN_NC, (n + 1) * FFN_NC)
            o_ref[rows, cols] += _dot(gated, w2_ref[:, cols])

    @pl.when(j == pl.num_programs(1) - 1)
    def _():
        h = x_ref[...] + (0.5 * mod_ref[0, k0 + 2:k0 + 3, :]) * o_ref[...]
        o_ref[...] = _rms(h, gf_ref[...]) if final else h


def _ffn(x, mod, row_of_tile, g, w1, w3, w2, k0, final_g=None):
    t = x.shape[0]
    final = final_g is not None
    gf = final_g if final else g
    return pl.pallas_call(
        functools.partial(_ffn_kernel, k0=k0, final=final),
        grid=(t // FFN_TM, D_FF // TF),
        in_specs=[
            pl.BlockSpec((FFN_TM, D_MODEL), lambda i, j: (i, 0), pipeline_mode=pl.Buffered(1 if final else 2)),
            _mod_spec(row_of_tile, FFN_TM),
            pl.BlockSpec((1, D_MODEL), lambda i, j: (0, 0)),
            pl.BlockSpec((D_MODEL, TF), lambda i, j: (0, j)),
            pl.BlockSpec((D_MODEL, TF), lambda i, j: (0, j)),
            pl.BlockSpec((TF, D_MODEL), lambda i, j: (j, 0)),
            pl.BlockSpec((1, D_MODEL), lambda i, j: (0, 0)),
        ],
        out_specs=pl.BlockSpec((FFN_TM, D_MODEL), lambda i, j: (i, 0)),
        out_shape=jax.ShapeDtypeStruct((t, D_MODEL), F32),
        scratch_shapes=[pltpu.VMEM((FFN_TM, D_MODEL), BF16)],
        compiler_params=_params(("parallel", "arbitrary")),
        name="ffn",
    )(x, mod, g.reshape(1, D_MODEL), w1, w3, w2, gf.reshape(1, D_MODEL))


def _mixer_in_kernel(x_ref, mod_ref, g_ref, wp1_ref, wp2_ref, gq_ref, gkv_ref, wa_ref, wb_ref, wk_ref, wv_ref,
                     cos_ref, sin_ref, q_ref, k_ref, v_ref, p2_ref, xn_ref, *, k0):
    _norm_modulate_rows(x_ref, g_ref, mod_ref, k0, xn_ref)
    xn = xn_ref[...]

    for n in range(3):
        cols = slice(n * D_NA, (n + 1) * D_NA)
        y2 = _dot(xn, wp2_ref[:, cols])
        p2_ref[:, cols] = (y2 * NA_SCALE if n == 0 else y2).astype(BF16)

    p1 = _dot(xn, wp1_ref[...])
    cos, sin = cos_ref[...], sin_ref[...]

    cqn = _rms(p1[:, :Q_LORA], gq_ref[...]).astype(BF16)
    ya = _dot(cqn, wa_ref[...])
    yb = _dot(cqn, wb_ref[...])
    for h in range(MLA_HEADS):
        lo = h * QK_PAD
        q_ref[:, lo:lo + LANES] = (ya[:, lo:lo + LANES] * MLA_SCALE).astype(BF16)
        rope = ya[:, lo + LANES:lo + QK_PAD] * cos + yb[:, h * LANES:(h + 1) * LANES] * sin
        q_ref[:, lo + LANES:lo + QK_PAD] = (rope * MLA_SCALE).astype(BF16)

    ckvn = _rms(p1[:, Q_LORA:Q_LORA + KV_LORA], gkv_ref[...]).astype(BF16)
    yk = _dot(ckvn, wk_ref[...])
    v_ref[...] = _dot(ckvn, wv_ref[...]).astype(BF16)
    kr_lo = Q_LORA + KV_LORA
    kr = (p1[:, kr_lo:kr_lo + LANES] * cos + p1[:, kr_lo + LANES:kr_lo + 2 * LANES] * sin).astype(BF16)
    for h in range(MLA_HEADS):
        lo = h * QK_PAD
        k_ref[:, lo:lo + LANES] = yk[:, h * LANES:(h + 1) * LANES].astype(BF16)
        k_ref[:, lo + LANES:lo + QK_PAD] = kr


def _mixer_in(x, mod, row_of_tile, table_block, g, w_p1, w_p2, gq, gkv, wa, wb, wk, wv, cos_t, sin_t, k0):
    t = x.shape[0]
    tm = MIX_TM

    def resident(shape):
        return pl.BlockSpec(shape, lambda i: (0,) * len(shape), pipeline_mode=pl.Buffered(1))

    return pl.pallas_call(
        functools.partial(_mixer_in_kernel, k0=k0),
        grid=(t // tm,),
        in_specs=[
            pl.BlockSpec((tm, D_MODEL), lambda i: (i, 0)),
            _mod_spec(row_of_tile, tm),
            resident((1, D_MODEL)),
            resident((D_MODEL, P1_COLS)),
            resident((D_MODEL, 3 * D_NA)),
            resident((1, Q_LORA)),
            resident((1, KV_LORA)),
            resident((Q_LORA, MLA_HEADS * QK_PAD)),
            resident((Q_LORA, MLA_HEADS * LANES)),
            resident((KV_LORA, MLA_HEADS * QK_NOPE)),
            resident((KV_LORA, D_MLA_OUT)),
            pl.BlockSpec((tm, LANES), lambda i: (table_block(i, tm), 0)),
            pl.BlockSpec((tm, LANES), lambda i: (table_block(i, tm), 0)),
        ],
        out_specs=[
            pl.BlockSpec((tm, MLA_HEADS * QK_PAD), lambda i: (i, 0)),
            pl.BlockSpec((tm, MLA_HEADS * QK_PAD), lambda i: (i, 0)),
            pl.BlockSpec((tm, D_MLA_OUT), lambda i: (i, 0)),
            pl.BlockSpec((tm, 3 * D_NA), lambda i: (i, 0)),
        ],
        out_shape=[
            jax.ShapeDtypeStruct((t, MLA_HEADS * QK_PAD), BF16),
            jax.ShapeDtypeStruct((t, MLA_HEADS * QK_PAD), BF16),
            jax.ShapeDtypeStruct((t, D_MLA_OUT), BF16),
            jax.ShapeDtypeStruct((t, 3 * D_NA), BF16),
        ],
        scratch_shapes=[pltpu.VMEM((tm, D_MODEL), BF16)],
        compiler_params=_params(("parallel",)),
        name="mixer_in",
    )(x, mod, g.reshape(1, D_MODEL), w_p1, w_p2, gq.reshape(1, Q_LORA), gkv.reshape(1, KV_LORA), wa, wb, wk, wv,
      cos_t, sin_t)


def _softmax_pv(s_parts, v_parts):
    m = functools.reduce(jnp.maximum, [jnp.max(s, axis=-1, keepdims=True) for s in s_parts])
    p_parts = [jnp.exp(s - m) for s in s_parts]
    l = functools.reduce(jnp.add, [jnp.sum(p, axis=-1, keepdims=True) for p in p_parts])
    o = functools.reduce(jnp.add, [_dot(p.astype(BF16), v) for p, v in zip(p_parts, v_parts)])
    return o * (1.0 / l)


def _mla_kernel(q_ref, kl_ref, kc_ref, vl_ref, vc_ref, o_ref):
    def scores(c):
        q = q_ref[c * MLA_QC:(c + 1) * MLA_QC, :]
        return [_dot_nt(q, kl_ref[...]), _dot_nt(q, kc_ref[...])]

    n_chunks = TQ // MLA_QC
    s = scores(0)
    for c in range(n_chunks):
        s_next = scores(c + 1) if c + 1 < n_chunks else None
        o_ref[c * MLA_QC:(c + 1) * MLA_QC, :] = _softmax_pv(s, [vl_ref[...], vc_ref[...]]).astype(o_ref.dtype)
        s = s_next


def _mla_latent(q, k_lat, k_ctx, v_lat, v_ctx):
    nq = SEQ // TQ
    return pl.pallas_call(
        _mla_kernel,
        grid=(BATCH, MLA_HEADS, nq),
        in_specs=[
            pl.BlockSpec((TQ, QK_PAD), lambda b, h, i: (b * nq + i, h)),
            pl.BlockSpec((SEQ, QK_PAD), lambda b, h, i: (b, h)),
            pl.BlockSpec((CTX_LEN, QK_PAD), lambda b, h, i: (b, h)),
            pl.BlockSpec((SEQ, V_HEAD), lambda b, h, i: (b, h)),
            pl.BlockSpec((CTX_LEN, V_HEAD), lambda b, h, i: (b, h)),
        ],
        out_specs=pl.BlockSpec((TQ, V_HEAD), lambda b, h, i: (b * nq + i, h)),
        out_shape=jax.ShapeDtypeStruct((BATCH * SEQ, D_MLA_OUT), BF16),
        compiler_params=_params(("parallel", "parallel", "arbitrary")),
        name="mla_latent",
    )(q, k_lat, k_ctx, v_lat, v_ctx)


def _ctx_attn_kernel(q_ref, k_ref, v_ref, o_ref, *, width):
    for h in range(MLA_HEADS):
        qk = slice(h * width, (h + 1) * width)
        vo = slice(h * V_HEAD, (h + 1) * V_HEAD)
        o_ref[:, vo] = _softmax_pv([_dot_nt(q_ref[:, qk], k_ref[:, qk])], [v_ref[:, vo]]).astype(o_ref.dtype)


def _ctx_attention(q, k, v, width, q_off, k_off, v_off):
    return pl.pallas_call(
        functools.partial(_ctx_attn_kernel, width=width),
        grid=(BATCH,),
        in_specs=[
            pl.BlockSpec((CTX_LEN, MLA_HEADS * width), lambda b: (b, q_off)),
            pl.BlockSpec((CTX_LEN, MLA_HEADS * width), lambda b: (b, k_off)),
            pl.BlockSpec((CTX_LEN, MLA_HEADS * V_HEAD), lambda b: (b, v_off)),
        ],
        out_specs=pl.BlockSpec((CTX_LEN, MLA_HEADS * V_HEAD), lambda b: (b, 0)),
        out_shape=jax.ShapeDtypeStruct((BATCH * CTX_LEN, MLA_HEADS * V_HEAD), BF16),
        compiler_params=_params(("parallel",)),
        name="ctx_attention",
    )(q, k, v)


def _na_group_window(g):
    if g == 0:
        return 0, 0
    if g == 1:
        return 1, 0
    if g == N_NA_GROUPS - 1:
        return 3, GRID_H - NA_WIN_ROWS
    return 2, NA_GROUP_ROWS * g - WIN_H // 2


def _na_kernel(q_ref, k_ref, v_ref, kc_ref, vc_ref, tab_ref, o_ref):
    gq = NA_GROUP_ROWS * GRID_W
    gk = NA_WIN_ROWS * GRID_W

    def scores(g):
        var, ks = _na_group_window(g)
        q = q_ref[g * gq:(g + 1) * gq, :]
        kw = k_ref[ks * GRID_W:ks * GRID_W + gk, :]
        return [_dot_nt(q, kw) + tab_ref[var], _dot_nt(q, kc_ref[...])]

    s = scores(0)
    for g in range(N_NA_GROUPS):
        s_next = scores(g + 1) if g + 1 < N_NA_GROUPS else None
        _, ks = _na_group_window(g)
        vw = v_ref[ks * GRID_W:ks * GRID_W + gk, :]
        o_ref[g * gq:(g + 1) * gq, :] = _softmax_pv(s, [vw, vc_ref[...]]).astype(o_ref.dtype)
        s = s_next


def _na_tables(rpb):
    nh, nr, nc = rpb.shape
    v = jnp.pad(rpb.astype(F32), ((0, 0), (0, 0), (GRID_W - WIN_W, GRID_W + WIN_W - nc)))
    flat = jnp.broadcast_to(v[:, :, None, :], (nh, nr, GRID_W, 2 * GRID_W)).reshape(nh, nr, 2 * GRID_W * GRID_W)
    toe = flat[:, :, :GRID_W * (2 * GRID_W - 1)].reshape(nh, nr, GRID_W, 2 * GRID_W - 1)[..., GRID_W - 1:]
    row_pad = WIN_H // 2
    toe = jnp.pad(toe, ((0, 0), (row_pad, row_pad), (0, 0), (0, 0)))

    qa = np.arange(NA_GROUP_ROWS).reshape(-1, 1, 1, 1)
    qc = np.arange(GRID_W).reshape(1, -1, 1, 1)
    ki = np.arange(NA_WIN_ROWS).reshape(1, 1, -1, 1)
    kc = np.arange(GRID_W).reshape(1, 1, 1, -1)
    col_start = np.clip(qc - WIN_W // 2, 0, GRID_W - WIN_W)
    col_ok = (kc >= col_start) & (kc < col_start + WIN_W)
    tables = []
    for g in (0, 1, 2, N_NA_GROUPS - 1):
        _, ks = _na_group_window(g)
        r = NA_GROUP_ROWS * g + qa
        rs = np.clip(r - WIN_H // 2, 0, GRID_H - WIN_H)
        krow = ks + ki
        ok = np.broadcast_to((krow >= rs) & (krow < rs + WIN_H) & col_ok,
                             (NA_GROUP_ROWS, GRID_W, NA_WIN_ROWS, GRID_W))
        per_row = []
        for a in range(NA_GROUP_ROWS):
            base = ks - (NA_GROUP_ROWS * g + a) + WIN_H - 1 + row_pad
            assert 0 <= base and base + NA_WIN_ROWS <= nr + 2 * row_pad
            per_row.append(toe[:, base:base + NA_WIN_ROWS])
        bias = jnp.stack(per_row, axis=1).transpose(0, 1, 3, 2, 4)
        bias = jnp.where(ok[None], bias, NEG)
        tables.append(bias.reshape(nh, NA_GROUP_ROWS * GRID_W, NA_WIN_ROWS * GRID_W))
    return jnp.stack(tables)


def _na_latent(p2_lat, p2_ctx, tables):
    nh = NA_HEADS
    gq = NA_GROUP_ROWS * GRID_W
    gk = NA_WIN_ROWS * GRID_W
    return pl.pallas_call(
        _na_kernel,
        grid=(nh, BATCH),
        in_specs=[
            pl.BlockSpec((SEQ, NA_HEAD_DIM), lambda h, b: (b, h)),
            pl.BlockSpec((SEQ, NA_HEAD_DIM), lambda h, b: (b, nh + h)),
            pl.BlockSpec((SEQ, NA_HEAD_DIM), lambda h, b: (b, 2 * nh + h)),
            pl.BlockSpec((CTX_LEN, NA_HEAD_DIM), lambda h, b: (b, nh + h)),
            pl.BlockSpec((CTX_LEN, NA_HEAD_DIM), lambda h, b: (b, 2 * nh + h)),
            pl.BlockSpec((4, None, gq, gk), lambda h, b: (0, h, 0, 0)),
        ],
        out_specs=pl.BlockSpec((SEQ, NA_HEAD_DIM), lambda h, b: (b, h)),
        out_shape=jax.ShapeDtypeStruct((BATCH * SEQ, D_NA), BF16),
        compiler_params=_params(("parallel", "parallel")),
        name="na_latent",
    )(p2_lat, p2_lat, p2_lat, p2_ctx, p2_ctx, tables)


def _merge_kernel(a_ref, b_ref, h_ref, mod_ref, ga_ref, gb_ref, wa_ref, wb_ref, o_ref, *, k0):
    an = _rms(a_ref[...].astype(F32), ga_ref[...]).astype(BF16)
    bn = _rms(b_ref[...].astype(F32), gb_ref[...]).astype(BF16)
    y = _dot(an, wa_ref[...]) + _dot(bn, wb_ref[...])
    o_ref[...] = h_ref[...] + mod_ref[0, k0:k0 + 1, :] * y


def _merge(a, b, h, mod, row_of_tile, out_norm, w_out, k0):
    t = h.shape[0]
    g2 = out_norm.reshape(2, 1, D_MLA_OUT)
    return pl.pallas_call(
        functools.partial(_merge_kernel, k0=k0),
        grid=(t // TM,),
        in_specs=[
            pl.BlockSpec((TM, D_MLA_OUT), lambda i: (i, 0)),
            pl.BlockSpec((TM, D_NA), lambda i: (i, 0)),
            pl.BlockSpec((TM, D_MODEL), lambda i: (i, 0)),
            _mod_spec(row_of_tile, TM),
            pl.BlockSpec((None, 1, D_MLA_OUT), lambda i: (0, 0, 0)),
            pl.BlockSpec((None, 1, D_NA), lambda i: (1, 0, 0)),
            pl.BlockSpec((D_MLA_OUT, D_MODEL), lambda i: (0, 0)),
            pl.BlockSpec((D_NA, D_MODEL), lambda i: (1, 0)),
        ],
        out_specs=pl.BlockSpec((TM, D_MODEL), lambda i: (i, 0)),
        out_shape=jax.ShapeDtypeStruct((t, D_MODEL), F32),
        compiler_params=_params(("parallel",)),
        name="merge",
    )(a, b, h, mod, g2, g2, w_out, w_out)


def _rope_tables():
    t = np.arange(SEQ)
    pos = np.stack([t // GRID_W, t % GRID_W], axis=-1).astype(np.float32)
    inv_freq = jnp.asarray(ROPE_THETA, F32) ** (-jnp.arange(N_FREQ, dtype=F32) / N_FREQ)
    ang = jnp.asarray(pos)[:, :, None] * inv_freq
    cos, sin = jnp.cos(ang), jnp.sin(ang)
    cos_t = jnp.concatenate([cos, cos], axis=-1).reshape(SEQ, QK_ROPE)
    sin_t = jnp.concatenate([-sin, sin], axis=-1).reshape(SEQ, QK_ROPE)
    pad = jnp.zeros((SEQ, LANES - QK_ROPE), F32)
    cos_t = jnp.concatenate([cos_t, pad], axis=-1)
    sin_t = jnp.concatenate([sin_t, pad], axis=-1)
    ident = jnp.concatenate([jnp.ones((TM, QK_ROPE), F32), jnp.zeros((TM, LANES - QK_ROPE), F32)], axis=-1)
    return (jnp.concatenate([cos_t, ident], axis=0),
            jnp.concatenate([sin_t, jnp.zeros((TM, LANES), F32)], axis=0))


def _rotate_half_cols(w):
    blocks = w.reshape(w.shape[:-1] + (2, 2, N_FREQ))
    return blocks[..., ::-1, :].reshape(w.shape)


def _layer_weights(w_in, w_uq, w_ukv):
    kr = w_in[:, 2 * Q_LORA:2 * Q_LORA + QK_ROPE]
    zpad = jnp.zeros((D_MODEL, LANES - QK_ROPE), F32)
    w_p1 = jnp.concatenate([w_in[:, :2 * Q_LORA], kr, zpad, _rotate_half_cols(kr), zpad], axis=1).astype(BF16)
    w_p2 = w_in[:, 2 * Q_LORA + QK_ROPE:].astype(BF16)
    uq = w_uq.reshape(Q_LORA, MLA_HEADS, QK_NOPE + QK_ROPE)
    hz = jnp.zeros((Q_LORA, MLA_HEADS, QK_PAD - QK_NOPE - QK_ROPE), F32)
    wa = jnp.concatenate([uq, hz], axis=-1).reshape(Q_LORA, MLA_HEADS * QK_PAD).astype(BF16)
    wb = jnp.concatenate([_rotate_half_cols(uq[..., QK_NOPE:]), hz], axis=-1)
    wb = wb.reshape(Q_LORA, MLA_HEADS * LANES).astype(BF16)
    ukv = w_ukv.reshape(KV_LORA, MLA_HEADS, QK_NOPE + V_HEAD)
    wk = ukv[..., :QK_NOPE].reshape(KV_LORA, MLA_HEADS * QK_NOPE).astype(BF16)
    wv = ukv[..., QK_NOPE:].reshape(KV_LORA, D_MLA_OUT).astype(BF16)
    return w_p1, w_p2, wa, wb, wk, wv


def kernel(x, c, ctx, c_ctx, ada_w, ada_b, ffn1_norm, ffn1_w1, ffn1_w3, ffn1_w2, mix_norm, w_in, qa_norm, w_uq,
           kva_norm, w_ukv, na_rpb, out_norm, w_out, ffn2_norm, ffn2_w1, ffn2_w3, ffn2_w2, final_norm):
    c_rows = jnp.concatenate([c, c_ctx[None, :], jnp.zeros((MOD_ROWS - BATCH - 1, D_MODEL), F32)], axis=0)
    mod = _modulation(c_rows, ada_w, ada_b)
    cos_t, sin_t = _rope_tables()
    lat_tab = lambda i, tm: i % (SEQ // tm)
    ctx_tab = lambda i, tm: SEQ // tm

    h = x.reshape(BATCH * SEQ, D_MODEL)
    hc = ctx.reshape(BATCH * CTX_LEN, D_MODEL)
    for l in range(DEPTH):
        last = l == DEPTH - 1
        lat_row = lambda i, tm, l=l: l * MOD_ROWS + (i * tm) // SEQ
        ctx_row = lambda i, tm, l=l: l * MOD_ROWS + CTX_MOD_ROW
        w1, w3, w2 = ffn1_w1[l].astype(BF16), ffn1_w3[l].astype(BF16), ffn1_w2[l].astype(BF16)
        h = _ffn(h, mod, lat_row, ffn1_norm[l], w1, w3, w2, 0)
        hc = _ffn(hc, mod, ctx_row, ffn1_norm[l], w1, w3, w2, 0)

        w_p1, w_p2, wa, wb, wk, wv = _layer_weights(w_in[l], w_uq[l], w_ukv[l])
        tables = _na_tables(na_rpb[l])
        mix_w = (mix_norm[l], w_p1, w_p2, qa_norm[l], kva_norm[l], wa, wb, wk, wv, cos_t, sin_t, 3)
        q, k, v, p2 = _mixer_in(h, mod, lat_row, lat_tab, *mix_w)
        qc, kc, vc, p2c = _mixer_in(hc, mod, ctx_row, ctx_tab, *mix_w)
        a_lat = _mla_latent(q, k, kc, v, vc)
        b_lat = _na_latent(p2, p2c, tables)
        h = _merge(a_lat, b_lat, h, mod, lat_row, out_norm[l], w_out[l].astype(BF16), 5)
        w1, w3, w2 = ffn2_w1[l].astype(BF16), ffn2_w3[l].astype(BF16), ffn2_w2[l].astype(BF16)
        h = _ffn(h, mod, lat_row, ffn2_norm[l], w1, w3, w2, 6, final_g=final_norm if last else None)
        if not last:
            a_ctx = _ctx_attention(qc, kc, vc, QK_PAD, 0, 0, 0)
            b_ctx = _ctx_attention(p2c, p2c, p2c, NA_HEAD_DIM, 0, 1, 2)
            hc = _merge(a_ctx, b_ctx, hc, mod, ctx_row, out_norm[l], w_out[l].astype(BF16), 5)
            hc = _ffn(hc, mod, ctx_row, ffn2_norm[l], w1, w3, w2, 6)
    return h.reshape(BATCH, SEQ, D_MODEL)
```

```python
import functools

import numpy as np
import jax
import jax.numpy as jnp
from jax import lax
from jax.experimental import pallas as pl
from jax.experimental.pallas import tpu as pltpu

D_MODEL = 2048
BATCH = 16
SEQ = 2048
DEPTH = 2
GRID_W = 64
GRID_H = SEQ // GRID_W
CTX_LEN = 256
MLA_HEADS = 8
QK_NOPE = 128
QK_ROPE = 64
V_HEAD = 128
Q_LORA = 512
KV_LORA = 512
N_FREQ = QK_ROPE // 4
ROPE_THETA = 10000.0
MLA_SCALE = (QK_NOPE + QK_ROPE) ** -0.5
NA_HEADS = 8
NA_HEAD_DIM = 128
WIN_H = 8
WIN_W = 16
NA_SCALE = NA_HEAD_DIM ** -0.5
D_MLA_OUT = MLA_HEADS * V_HEAD
D_NA = NA_HEADS * NA_HEAD_DIM
D_FF = 5632
N_MOD = 9
EPS = 1e-6
NEG = -1e30

F32 = jnp.float32
BF16 = jnp.bfloat16

LANES = 128
V7X_VMEM_LIMIT = 60000 * 1024

MOD_ROWS = 24
CTX_MOD_ROW = BATCH
QK_PAD = 256
P1_COLS = 2 * Q_LORA + 2 * LANES
NA_GROUP_ROWS = 4
NA_WIN_ROWS = 12
N_NA_GROUPS = GRID_H // NA_GROUP_ROWS

ROW_CHUNK = 16
MIX_TM = 256
TM = 512
FFN_TM = 1024
FFN_MC = 512
FFN_NC = 512
TF = 512
TQ = 2048
MLA_QC = 512


def _params(sem, vmem=V7X_VMEM_LIMIT):
    return pltpu.CompilerParams(dimension_semantics=sem, vmem_limit_bytes=vmem)


def _rms(x, g):
    return x * lax.rsqrt(jnp.mean(x * x, axis=-1, keepdims=True) + EPS) * g


def _silu(a):
    return a * (1.0 / (1.0 + jnp.exp(-a)))


def _dot(a, b):
    return jnp.dot(a, b, preferred_element_type=F32)


def _dot_nt(a, b):
    return lax.dot_general(a, b, (((1,), (1,)), ((), ())), preferred_element_type=F32)


def _mod_kernel(s_ref, w_ref, b_ref, o_ref):
    s = _silu(s_ref[...])
    o_ref[...] = _dot(s.astype(BF16), w_ref[...].astype(BF16)) + b_ref[...]


def _modulation(c_rows, ada_w, ada_b):
    tn = 1024
    n = N_MOD * D_MODEL
    out = pl.pallas_call(
        _mod_kernel,
        grid=(DEPTH, n // tn),
        in_specs=[
            pl.BlockSpec((MOD_ROWS, D_MODEL), lambda l, j: (0, 0)),
            pl.BlockSpec((None, D_MODEL, tn), lambda l, j: (l, 0, j)),
            pl.BlockSpec((None, 1, tn), lambda l, j: (l, 0, j)),
        ],
        out_specs=pl.BlockSpec((None, MOD_ROWS, tn), lambda l, j: (l, 0, j)),
        out_shape=jax.ShapeDtypeStruct((DEPTH, MOD_ROWS, n), F32),
        compiler_params=_params(("parallel", "parallel")),
        name="adaln_mod",
    )(c_rows, ada_w, ada_b.reshape(DEPTH, 1, n))
    return out.reshape(DEPTH * MOD_ROWS, N_MOD, D_MODEL)


def _norm_modulate_rows(x_ref, g_ref, mod_ref, k0, xn_ref):
    gain = g_ref[...] * (1.0 + mod_ref[0, k0 + 1:k0 + 2, :])
    shift = mod_ref[0, k0:k0 + 1, :]

    def body(r, carry):
        rows = pl.ds(pl.multiple_of(r * ROW_CHUNK, ROW_CHUNK), ROW_CHUNK)
        x = x_ref[rows, :]
        y = x * lax.rsqrt(jnp.mean(x * x, axis=-1, keepdims=True) + EPS)
        xn_ref[rows, :] = (y * gain + shift).astype(BF16)
        return carry

    lax.fori_loop(0, x_ref.shape[0] // ROW_CHUNK, body, 0, unroll=8)


def _mod_spec(row_of_tile, tm):
    return pl.BlockSpec((1, N_MOD, D_MODEL), lambda i, *_: (row_of_tile(i, tm), 0, 0))


def _ffn_kernel(x_ref, mod_ref, g_ref, w1_ref, w3_ref, w2_ref, gf_ref, o_ref, xn_ref, *, k0, final):
    j = pl.program_id(1)

    def step(first):
        if first:
            gain = g_ref[...] * (1.0 + mod_ref[0, k0 + 1:k0 + 2, :])
            shift = mod_ref[0, k0:k0 + 1, :]
        for m in range(FFN_TM // FFN_MC):
            if first:
                for r in range(m * FFN_MC, (m + 1) * FFN_MC, ROW_CHUNK):
                    x = x_ref[r:r + ROW_CHUNK, :]
                    y = x * lax.rsqrt(jnp.mean(x * x, axis=-1, keepdims=True) + EPS)
                    xn_ref[r:r + ROW_CHUNK, :] = (y * gain + shift).astype(BF16)
            rows = slice(m * FFN_MC, (m + 1) * FFN_MC)
            xn = xn_ref[rows, :]
            a = _dot(xn, w1_ref[...])
            b = _dot(xn, w3_ref[...])
            gated = (_silu(a) * b).astype(BF16)
            for n in range(D_MODEL // FFN_NC):
                cols = slice(n * FFN_NC, (n + 1) * FFN_NC)
                part = _dot(gated, w2_ref[:, cols])
                if first:
                    o_ref[rows, cols] = part
                else:
                    o_ref[rows, cols] += part

    pl.when(j == 0)(functools.partial(step, True))
    pl.when(j > 0)(functools.partial(step, False))

    @pl.when(j == pl.num_programs(1) - 1)
    def _():
        half_gate = 0.5 * mod_ref[0, k0 + 2:k0 + 3, :]
        if final:
            for r in range(0, FFN_TM, ROW_CHUNK):
                rows = slice(r, r + ROW_CHUNK)
                o_ref[rows, :] = _rms(x_ref[rows, :] + half_gate * o_ref[rows, :], gf_ref[...])
        else:
            o_ref[...] = x_ref[...] + half_gate * o_ref[...]


def _ffn(x, mod, row_of_tile, g, w1, w3, w2, k0, final_g=None):
    t = x.shape[0]
    final = final_g is not None
    gf = final_g if final else g
    return pl.pallas_call(
        functools.partial(_ffn_kernel, k0=k0, final=final),
        grid=(t // FFN_TM, D_FF // TF),
        in_specs=[
            pl.BlockSpec((FFN_TM, D_MODEL), lambda i, j: (i, 0)),
            _mod_spec(row_of_tile, FFN_TM),
            pl.BlockSpec((1, D_MODEL), lambda i, j: (0, 0)),
            pl.BlockSpec((D_MODEL, TF), lambda i, j: (0, j)),
            pl.BlockSpec((D_MODEL, TF), lambda i, j: (0, j)),
            pl.BlockSpec((TF, D_MODEL), lambda i, j: (j, 0)),
            pl.BlockSpec((1, D_MODEL), lambda i, j: (0, 0)),
        ],
        out_specs=pl.BlockSpec((FFN_TM, D_MODEL), lambda i, j: (i, 0)),
        out_shape=jax.ShapeDtypeStruct((t, D_MODEL), F32),
        scratch_shapes=[pltpu.VMEM((FFN_TM, D_MODEL), BF16)],
        compiler_params=_params(("parallel", "arbitrary")),
        name="ffn",
    )(x, mod, g.reshape(1, D_MODEL), w1, w3, w2, gf.reshape(1, D_MODEL))


def _mixer_in_kernel(x_ref, mod_ref, g_ref, wp1_ref, wp2_ref, gq_ref, gkv_ref, wa_ref, wb_ref, wk_ref, wv_ref,
                     cos_ref, sin_ref, q_ref, k_ref, v_ref, p2_ref, xn_ref, *, k0):
    _norm_modulate_rows(x_ref, g_ref, mod_ref, k0, xn_ref)
    xn = xn_ref[...]

    for n in range(3):
        cols = slice(n * D_NA, (n + 1) * D_NA)
        y2 = _dot(xn, wp2_ref[:, cols])
        p2_ref[:, cols] = (y2 * NA_SCALE if n == 0 else y2).astype(BF16)

    p1 = _dot(xn, wp1_ref[...])
    cos, sin = cos_ref[...], sin_ref[...]

    cqn = _rms(p1[:, :Q_LORA], gq_ref[...]).astype(BF16)
    ya = _dot(cqn, wa_ref[...])
    yb = _dot(cqn, wb_ref[...])
    for h in range(MLA_HEADS):
        lo = h * QK_PAD
        q_ref[:, lo:lo + LANES] = (ya[:, lo:lo + LANES] * MLA_SCALE).astype(BF16)
        rope = ya[:, lo + LANES:lo + QK_PAD] * cos + yb[:, h * LANES:(h + 1) * LANES] * sin
        q_ref[:, lo + LANES:lo + QK_PAD] = (rope * MLA_SCALE).astype(BF16)

    ckvn = _rms(p1[:, Q_LORA:Q_LORA + KV_LORA], gkv_ref[...]).astype(BF16)
    yk = _dot(ckvn, wk_ref[...])
    v_ref[...] = _dot(ckvn, wv_ref[...]).astype(BF16)
    kr_lo = Q_LORA + KV_LORA
    kr = (p1[:, kr_lo:kr_lo + LANES] * cos + p1[:, kr_lo + LANES:kr_lo + 2 * LANES] * sin).astype(BF16)
    for h in range(MLA_HEADS):
        lo = h * QK_PAD
        k_ref[:, lo:lo + LANES] = yk[:, h * LANES:(h + 1) * LANES].astype(BF16)
        k_ref[:, lo + LANES:lo + QK_PAD] = kr


def _mixer_in(x, mod, row_of_tile, table_block, g, w_p1, w_p2, gq, gkv, wa, wb, wk, wv, cos_t, sin_t, k0):
    t = x.shape[0]
    tm = MIX_TM

    def resident(shape):
        return pl.BlockSpec(shape, lambda i: (0,) * len(shape), pipeline_mode=pl.Buffered(1))

    return pl.pallas_call(
        functools.partial(_mixer_in_kernel, k0=k0),
        grid=(t // tm,),
        in_specs=[
            pl.BlockSpec((tm, D_MODEL), lambda i: (i, 0)),
            _mod_spec(row_of_tile, tm),
            resident((1, D_MODEL)),
            resident((D_MODEL, P1_COLS)),
            resident((D_MODEL, 3 * D_NA)),
            resident((1, Q_LORA)),
            resident((1, KV_LORA)),
            resident((Q_LORA, MLA_HEADS * QK_PAD)),
            resident((Q_LORA, MLA_HEADS * LANES)),
            resident((KV_LORA, MLA_HEADS * QK_NOPE)),
            resident((KV_LORA, D_MLA_OUT)),
            pl.BlockSpec((tm, LANES), lambda i: (table_block(i, tm), 0)),
            pl.BlockSpec((tm, LANES), lambda i: (table_block(i, tm), 0)),
        ],
        out_specs=[
            pl.BlockSpec((tm, MLA_HEADS * QK_PAD), lambda i: (i, 0)),
            pl.BlockSpec((tm, MLA_HEADS * QK_PAD), lambda i: (i, 0)),
            pl.BlockSpec((tm, D_MLA_OUT), lambda i: (i, 0)),
            pl.BlockSpec((tm, 3 * D_NA), lambda i: (i, 0)),
        ],
        out_shape=[
            jax.ShapeDtypeStruct((t, MLA_HEADS * QK_PAD), BF16),
            jax.ShapeDtypeStruct((t, MLA_HEADS * QK_PAD), BF16),
            jax.ShapeDtypeStruct((t, D_MLA_OUT), BF16),
            jax.ShapeDtypeStruct((t, 3 * D_NA), BF16),
        ],
        scratch_shapes=[pltpu.VMEM((tm, D_MODEL), BF16)],
        compiler_params=_params(("parallel",)),
        name="mixer_in",
    )(x, mod, g.reshape(1, D_MODEL), w_p1, w_p2, gq.reshape(1, Q_LORA), gkv.reshape(1, KV_LORA), wa, wb, wk, wv,
      cos_t, sin_t)


def _softmax_pv(s_parts, v_parts):
    m = functools.reduce(jnp.maximum, [jnp.max(s, axis=-1, keepdims=True) for s in s_parts])
    p_parts = [jnp.exp(s - m) for s in s_parts]
    l = functools.reduce(jnp.add, [jnp.sum(p, axis=-1, keepdims=True) for p in p_parts])
    o = functools.reduce(jnp.add, [_dot(p.astype(BF16), v) for p, v in zip(p_parts, v_parts)])
    return o * (1.0 / l)


def _mla_kernel(q_ref, kl_ref, kc_ref, vl_ref, vc_ref, o_ref):
    def scores(c):
        q = q_ref[c * MLA_QC:(c + 1) * MLA_QC, :]
        return [_dot_nt(q, kl_ref[...]), _dot_nt(q, kc_ref[...])]

    n_chunks = TQ // MLA_QC
    s = scores(0)
    for c in range(n_chunks):
        s_next = scores(c + 1) if c + 1 < n_chunks else None
        o_ref[c * MLA_QC:(c + 1) * MLA_QC, :] = _softmax_pv(s, [vl_ref[...], vc_ref[...]]).astype(o_ref.dtype)
        s = s_next


def _mla_latent(q, k_lat, k_ctx, v_lat, v_ctx):
    nq = SEQ // TQ
    return pl.pallas_call(
        _mla_kernel,
        grid=(BATCH, MLA_HEADS, nq),
        in_specs=[
            pl.BlockSpec((TQ, QK_PAD), lambda b, h, i: (b * nq + i, h)),
            pl.BlockSpec((SEQ, QK_PAD), lambda b, h, i: (b, h)),
            pl.BlockSpec((CTX_LEN, QK_PAD), lambda b, h, i: (b, h)),
            pl.BlockSpec((SEQ, V_HEAD), lambda b, h, i: (b, h)),
            pl.BlockSpec((CTX_LEN, V_HEAD), lambda b, h, i: (b, h)),
        ],
        out_specs=pl.BlockSpec((TQ, V_HEAD), lambda b, h, i: (b * nq + i, h)),
        out_shape=jax.ShapeDtypeStruct((BATCH * SEQ, D_MLA_OUT), BF16),
        compiler_params=_params(("parallel", "parallel", "arbitrary")),
        name="mla_latent",
    )(q, k_lat, k_ctx, v_lat, v_ctx)


def _ctx_attn_kernel(q_ref, k_ref, v_ref, o_ref, *, width):
    for h in range(MLA_HEADS):
        qk = slice(h * width, (h + 1) * width)
        vo = slice(h * V_HEAD, (h + 1) * V_HEAD)
        o_ref[:, vo] = _softmax_pv([_dot_nt(q_ref[:, qk], k_ref[:, qk])], [v_ref[:, vo]]).astype(o_ref.dtype)


def _ctx_attention(q, k, v, width, q_off, k_off, v_off):
    return pl.pallas_call(
        functools.partial(_ctx_attn_kernel, width=width),
        grid=(BATCH,),
        in_specs=[
            pl.BlockSpec((CTX_LEN, MLA_HEADS * width), lambda b: (b, q_off)),
            pl.BlockSpec((CTX_LEN, MLA_HEADS * width), lambda b: (b, k_off)),
            pl.BlockSpec((CTX_LEN, MLA_HEADS * V_HEAD), lambda b: (b, v_off)),
        ],
        out_specs=pl.BlockSpec((CTX_LEN, MLA_HEADS * V_HEAD), lambda b: (b, 0)),
        out_shape=jax.ShapeDtypeStruct((BATCH * CTX_LEN, MLA_HEADS * V_HEAD), BF16),
        compiler_params=_params(("parallel",)),
        name="ctx_attention",
    )(q, k, v)


def _na_group_window(g):
    if g == 0:
        return 0, 0
    if g == 1:
        return 1, 0
    if g == N_NA_GROUPS - 1:
        return 3, GRID_H - NA_WIN_ROWS
    return 2, NA_GROUP_ROWS * g - WIN_H // 2


def _na_kernel(q_ref, k_ref, v_ref, kc_ref, vc_ref, tab_ref, o_ref):
    gq = NA_GROUP_ROWS * GRID_W
    gk = NA_WIN_ROWS * GRID_W

    def scores(g):
        var, ks = _na_group_window(g)
        q = q_ref[g * gq:(g + 1) * gq, :]
        kw = k_ref[ks * GRID_W:ks * GRID_W + gk, :]
        return [_dot_nt(q, kw) + tab_ref[var], _dot_nt(q, kc_ref[...])]

    s = scores(0)
    for g in range(N_NA_GROUPS):
        s_next = scores(g + 1) if g + 1 < N_NA_GROUPS else None
        _, ks = _na_group_window(g)
        vw = v_ref[ks * GRID_W:ks * GRID_W + gk, :]
        o_ref[g * gq:(g + 1) * gq, :] = _softmax_pv(s, [vw, vc_ref[...]]).astype(o_ref.dtype)
        s = s_next


def _na_tables(rpb):
    nh, nr, nc = rpb.shape
    v = jnp.pad(rpb.astype(F32), ((0, 0), (0, 0), (GRID_W - WIN_W, GRID_W + WIN_W - nc)))
    flat = jnp.broadcast_to(v[:, :, None, :], (nh, nr, GRID_W, 2 * GRID_W)).reshape(nh, nr, 2 * GRID_W * GRID_W)
    toe = flat[:, :, :GRID_W * (2 * GRID_W - 1)].reshape(nh, nr, GRID_W, 2 * GRID_W - 1)[..., GRID_W - 1:]
    row_pad = WIN_H // 2
    toe = jnp.pad(toe, ((0, 0), (row_pad, row_pad), (0, 0), (0, 0)))

    qa = np.arange(NA_GROUP_ROWS).reshape(-1, 1, 1, 1)
    qc = np.arange(GRID_W).reshape(1, -1, 1, 1)
    ki = np.arange(NA_WIN_ROWS).reshape(1, 1, -1, 1)
    kc = np.arange(GRID_W).reshape(1, 1, 1, -1)
    col_start = np.clip(qc - WIN_W // 2, 0, GRID_W - WIN_W)
    col_ok = (kc >= col_start) & (kc < col_start + WIN_W)
    tables = []
    for g in (0, 1, 2, N_NA_GROUPS - 1):
        _, ks = _na_group_window(g)
        r = NA_GROUP_ROWS * g + qa
        rs = np.clip(r - WIN_H // 2, 0, GRID_H - WIN_H)
        krow = ks + ki
        ok = np.broadcast_to((krow >= rs) & (krow < rs + WIN_H) & col_ok,
                             (NA_GROUP_ROWS, GRID_W, NA_WIN_ROWS, GRID_W))
        per_row = []
        for a in range(NA_GROUP_ROWS):
            base = ks - (NA_GROUP_ROWS * g + a) + WIN_H - 1 + row_pad
            assert 0 <= base and base + NA_WIN_ROWS <= nr + 2 * row_pad
            per_row.append(toe[:, base:base + NA_WIN_ROWS])
        bias = jnp.stack(per_row, axis=1).transpose(0, 1, 3, 2, 4)
        bias = jnp.where(ok[None], bias, NEG)
        tables.append(bias.reshape(nh, NA_GROUP_ROWS * GRID_W, NA_WIN_ROWS * GRID_W))
    return jnp.stack(tables)


def _na_latent(p2_lat, p2_ctx, tables):
    nh = NA_HEADS
    gq = NA_GROUP_ROWS * GRID_W
    gk = NA_WIN_ROWS * GRID_W
    return pl.pallas_call(
        _na_kernel,
        grid=(nh, BATCH),
        in_specs=[
            pl.BlockSpec((SEQ, NA_HEAD_DIM), lambda h, b: (b, h)),
            pl.BlockSpec((SEQ, NA_HEAD_DIM), lambda h, b: (b, nh + h)),
            pl.BlockSpec((SEQ, NA_HEAD_DIM), lambda h, b: (b, 2 * nh + h)),
            pl.BlockSpec((CTX_LEN, NA_HEAD_DIM), lambda h, b: (b, nh + h)),
            pl.BlockSpec((CTX_LEN, NA_HEAD_DIM), lambda h, b: (b, 2 * nh + h)),
            pl.BlockSpec((4, None, gq, gk), lambda h, b: (0, h, 0, 0)),
        ],
        out_specs=pl.BlockSpec((SEQ, NA_HEAD_DIM), lambda h, b: (b, h)),
        out_shape=jax.ShapeDtypeStruct((BATCH * SEQ, D_NA), BF16),
        compiler_params=_params(("parallel", "parallel")),
        name="na_latent",
    )(p2_lat, p2_lat, p2_lat, p2_ctx, p2_ctx, tables)


def _merge_kernel(a_ref, b_ref, h_ref, mod_ref, ga_ref, gb_ref, wa_ref, wb_ref, o_ref, *, k0):
    an = _rms(a_ref[...].astype(F32), ga_ref[...]).astype(BF16)
    bn = _rms(b_ref[...].astype(F32), gb_ref[...]).astype(BF16)
    y = _dot(an, wa_ref[...]) + _dot(bn, wb_ref[...])
    o_ref[...] = h_ref[...] + mod_ref[0, k0:k0 + 1, :] * y


def _merge(a, b, h, mod, row_of_tile, out_norm, w_out, k0):
    t = h.shape[0]
    g2 = out_norm.reshape(2, 1, D_MLA_OUT)
    return pl.pallas_call(
        functools.partial(_merge_kernel, k0=k0),
        grid=(t // TM,),
        in_specs=[
            pl.BlockSpec((TM, D_MLA_OUT), lambda i: (i, 0)),
            pl.BlockSpec((TM, D_NA), lambda i: (i, 0)),
            pl.BlockSpec((TM, D_MODEL), lambda i: (i, 0)),
            _mod_spec(row_of_tile, TM),
            pl.BlockSpec((None, 1, D_MLA_OUT), lambda i: (0, 0, 0)),
            pl.BlockSpec((None, 1, D_NA), lambda i: (1, 0, 0)),
            pl.BlockSpec((D_MLA_OUT, D_MODEL), lambda i: (0, 0)),
            pl.BlockSpec((D_NA, D_MODEL), lambda i: (1, 0)),
        ],
        out_specs=pl.BlockSpec((TM, D_MODEL), lambda i: (i, 0)),
        out_shape=jax.ShapeDtypeStruct((t, D_MODEL), F32),
        compiler_params=_params(("parallel",)),
        name="merge",
    )(a, b, h, mod, g2, g2, w_out, w_out)


def _rope_tables():
    t = np.arange(SEQ)
    pos = np.stack([t // GRID_W, t % GRID_W], axis=-1).astype(np.float32)
    inv_freq = jnp.asarray(ROPE_THETA, F32) ** (-jnp.arange(N_FREQ, dtype=F32) / N_FREQ)
    ang = jnp.asarray(pos)[:, :, None] * inv_freq
    cos, sin = jnp.cos(ang), jnp.sin(ang)
    cos_t = jnp.concatenate([cos, cos], axis=-1).reshape(SEQ, QK_ROPE)
    sin_t = jnp.concatenate([-sin, sin], axis=-1).reshape(SEQ, QK_ROPE)
    pad = jnp.zeros((SEQ, LANES - QK_ROPE), F32)
    cos_t = jnp.concatenate([cos_t, pad], axis=-1)
    sin_t = jnp.concatenate([sin_t, pad], axis=-1)
    ident = jnp.concatenate([jnp.ones((TM, QK_ROPE), F32), jnp.zeros((TM, LANES - QK_ROPE), F32)], axis=-1)
    return (jnp.concatenate([cos_t, ident], axis=0),
            jnp.concatenate([sin_t, jnp.zeros((TM, LANES), F32)], axis=0))


def _rotate_half_cols(w):
    blocks = w.reshape(w.shape[:-1] + (2, 2, N_FREQ))
    return blocks[..., ::-1, :].reshape(w.shape)


def _layer_weights(w_in, w_uq, w_ukv):
    kr = w_in[:, 2 * Q_LORA:2 * Q_LORA + QK_ROPE]
    zpad = jnp.zeros((D_MODEL, LANES - QK_ROPE), F32)
    w_p1 = jnp.concatenate([w_in[:, :2 * Q_LORA], kr, zpad, _rotate_half_cols(kr), zpad], axis=1).astype(BF16)
    w_p2 = w_in[:, 2 * Q_LORA + QK_ROPE:].astype(BF16)
    uq = w_uq.reshape(Q_LORA, MLA_HEADS, QK_NOPE + QK_ROPE)
    hz = jnp.zeros((Q_LORA, MLA_HEADS, QK_PAD - QK_NOPE - QK_ROPE), F32)
    wa = jnp.concatenate([uq, hz], axis=-1).reshape(Q_LORA, MLA_HEADS * QK_PAD).astype(BF16)
    wb = jnp.concatenate([_rotate_half_cols(uq[..., QK_NOPE:]), hz], axis=-1)
    wb = wb.reshape(Q_LORA, MLA_HEADS * LANES).astype(BF16)
    ukv = w_ukv.reshape(KV_LORA, MLA_HEADS, QK_NOPE + V_HEAD)
    wk = ukv[..., :QK_NOPE].reshape(KV_LORA, MLA_HEADS * QK_NOPE).astype(BF16)
    wv = ukv[..., QK_NOPE:].reshape(KV_LORA, D_MLA_OUT).astype(BF16)
    return w_p1, w_p2, wa, wb, wk, wv


def kernel(x, c, ctx, c_ctx, ada_w, ada_b, ffn1_norm, ffn1_w1, ffn1_w3, ffn1_w2, mix_norm, w_in, qa_norm, w_uq,
           kva_norm, w_ukv, na_rpb, out_norm, w_out, ffn2_norm, ffn2_w1, ffn2_w3, ffn2_w2, final_norm):
    c_rows = jnp.concatenate([c, c_ctx[None, :], jnp.zeros((MOD_ROWS - BATCH - 1, D_MODEL), F32)], axis=0)
    mod = _modulation(c_rows, ada_w, ada_b)
    cos_t, sin_t = _rope_tables()
    lat_tab = lambda i, tm: i % (SEQ // tm)
    ctx_tab = lambda i, tm: SEQ // tm

    h = x.reshape(BATCH * SEQ, D_MODEL)
    hc = ctx.reshape(BATCH * CTX_LEN, D_MODEL)
    for l in range(DEPTH):
        last = l == DEPTH - 1
        lat_row = lambda i, tm, l=l: l * MOD_ROWS + (i * tm) // SEQ
        ctx_row = lambda i, tm, l=l: l * MOD_ROWS + CTX_MOD_ROW
        w1, w3, w2 = ffn1_w1[l].astype(BF16), ffn1_w3[l].astype(BF16), ffn1_w2[l].astype(BF16)
        h = _ffn(h, mod, lat_row, ffn1_norm[l], w1, w3, w2, 0)
        hc = _ffn(hc, mod, ctx_row, ffn1_norm[l], w1, w3, w2, 0)

        w_p1, w_p2, wa, wb, wk, wv = _layer_weights(w_in[l], w_uq[l], w_ukv[l])
        tables = _na_tables(na_rpb[l])
        mix_w = (mix_norm[l], w_p1, w_p2, qa_norm[l], kva_norm[l], wa, wb, wk, wv, cos_t, sin_t, 3)
        q, k, v, p2 = _mixer_in(h, mod, lat_row, lat_tab, *mix_w)
        qc, kc, vc, p2c = _mixer_in(hc, mod, ctx_row, ctx_tab, *mix_w)
        a_lat = _mla_latent(q, k, kc, v, vc)
        b_lat = _na_latent(p2, p2c, tables)
        h = _merge(a_lat, b_lat, h, mod, lat_row, out_norm[l], w_out[l].astype(BF16), 5)
        w1, w3, w2 = ffn2_w1[l].astype(BF16), ffn2_w3[l].astype(BF16), ffn2_w2[l].astype(BF16)
        h = _ffn(h, mod, lat_row, ffn2_norm[l], w1, w3, w2, 6, final_g=final_norm if last else None)
        if not last:
            a_ctx = _ctx_attention(qc, kc, vc, QK_PAD, 0, 0, 0)
            b_ctx = _ctx_attention(p2c, p2c, p2c, NA_HEAD_DIM, 0, 1, 2)
            hc = _merge(a_ctx, b_ctx, hc, mod, ctx_row, out_norm[l], w_out[l].astype(BF16), 5)
            hc = _ffn(hc, mod, ctx_row, ffn2_norm[l], w1, w3, w2, 6)
    return h.reshape(BATCH, SEQ, D_MODEL)
```

```python
import functools

import numpy as np
import jax
import jax.numpy as jnp
from jax import lax
from jax.experimental import pallas as pl
from jax.experimental.pallas import tpu as pltpu

D_MODEL = 2048
BATCH = 16
SEQ = 2048
DEPTH = 2
GRID_W = 64
GRID_H = SEQ // GRID_W
CTX_LEN = 256
MLA_HEADS = 8
QK_NOPE = 128
QK_ROPE = 64
V_HEAD = 128
Q_LORA = 512
KV_LORA = 512
N_FREQ = QK_ROPE // 4
ROPE_THETA = 10000.0
MLA_SCALE = (QK_NOPE + QK_ROPE) ** -0.5
NA_HEADS = 8
NA_HEAD_DIM = 128
WIN_H = 8
WIN_W = 16
NA_SCALE = NA_HEAD_DIM ** -0.5
LOG2E = 1.4426950408889634
MLA_QSCALE = MLA_SCALE * LOG2E
NA_QSCALE = NA_SCALE * LOG2E
D_MLA_OUT = MLA_HEADS * V_HEAD
D_NA = NA_HEADS * NA_HEAD_DIM
D_FF = 5632
N_MOD = 9
EPS = 1e-6
NEG = -1e30

F32 = jnp.float32
BF16 = jnp.bfloat16

LANES = 128
V7X_VMEM_LIMIT = 60000 * 1024

MOD_ROWS = 24
CTX_MOD_ROW = BATCH
QK_PAD = 256
P1_COLS = 2 * Q_LORA + 2 * LANES
NA_GROUP_ROWS = 4
NA_WIN_ROWS = 12
N_NA_GROUPS = GRID_H // NA_GROUP_ROWS

CAST_BLOCK_ELEMS = 1024 * 1024
ROW_CHUNK = 16
MIX_TM = 256
TM = 512
FFN_TM = 1024
FFN_MC = 512
FFN_NC = 512
TF = 512
TQ = 2048
MLA_QC = 512


def _params(sem, vmem=V7X_VMEM_LIMIT):
    return pltpu.CompilerParams(dimension_semantics=sem, vmem_limit_bytes=vmem)


def _rms(x, g):
    return x * lax.rsqrt(jnp.mean(x * x, axis=-1, keepdims=True) + EPS) * g


def _silu(a):
    return a * (1.0 / (1.0 + jnp.exp(-a)))


def _dot(a, b):
    return jnp.dot(a, b, preferred_element_type=F32)


def _dot_nt(a, b):
    return lax.dot_general(a, b, (((1,), (1,)), ((), ())), preferred_element_type=F32)


def _mod_kernel(s_ref, w_ref, b_ref, o_ref):
    s = _silu(s_ref[...])
    o_ref[...] = _dot(s.astype(BF16), w_ref[...].astype(BF16)) + b_ref[...]


def _modulation(c_rows, ada_w, ada_b):
    tn = 1024
    n = N_MOD * D_MODEL
    out = pl.pallas_call(
        _mod_kernel,
        grid=(DEPTH, n // tn),
        in_specs=[
            pl.BlockSpec((MOD_ROWS, D_MODEL), lambda l, j: (0, 0)),
            pl.BlockSpec((None, D_MODEL, tn), lambda l, j: (l, 0, j)),
            pl.BlockSpec((None, 1, tn), lambda l, j: (l, 0, j)),
        ],
        out_specs=pl.BlockSpec((None, MOD_ROWS, tn), lambda l, j: (l, 0, j)),
        out_shape=jax.ShapeDtypeStruct((DEPTH, MOD_ROWS, n), F32),
        compiler_params=_params(("parallel", "parallel")),
        name="adaln_mod",
    )(c_rows, ada_w, ada_b.reshape(DEPTH, 1, n))
    return out.reshape(DEPTH * MOD_ROWS, N_MOD, D_MODEL)


def _norm_modulate_rows(x_ref, g_ref, mod_ref, k0, xn_ref):
    gain = g_ref[...] * (1.0 + mod_ref[0, k0 + 1:k0 + 2, :])
    shift = mod_ref[0, k0:k0 + 1, :]

    def body(r, carry):
        rows = pl.ds(pl.multiple_of(r * ROW_CHUNK, ROW_CHUNK), ROW_CHUNK)
        x = x_ref[rows, :]
        y = x * lax.rsqrt(jnp.mean(x * x, axis=-1, keepdims=True) + EPS)
        xn_ref[rows, :] = (y * gain + shift).astype(BF16)
        return carry

    lax.fori_loop(0, x_ref.shape[0] // ROW_CHUNK, body, 0, unroll=8)


def _mod_spec(row_of_tile, tm):
    return pl.BlockSpec((1, N_MOD, D_MODEL), lambda i, *_: (row_of_tile(i, tm), 0, 0))


def _ffn_kernel(x_ref, mod_ref, g_ref, w1_ref, w3_ref, w2_ref, gf_ref, o_ref, xn_ref, *, k0, final):
    j = pl.program_id(1)

    def step(first):
        if first:
            gain = g_ref[...] * (1.0 + mod_ref[0, k0 + 1:k0 + 2, :])
            shift = mod_ref[0, k0:k0 + 1, :]
        for m in range(FFN_TM // FFN_MC):
            if first:
                for r in range(m * FFN_MC, (m + 1) * FFN_MC, ROW_CHUNK):
                    x = x_ref[r:r + ROW_CHUNK, :]
                    y = x * lax.rsqrt(jnp.mean(x * x, axis=-1, keepdims=True) + EPS)
                    xn_ref[r:r + ROW_CHUNK, :] = (y * gain + shift).astype(BF16)
            rows = slice(m * FFN_MC, (m + 1) * FFN_MC)
            xn = xn_ref[rows, :]
            a = _dot(xn, w1_ref[...])
            b = _dot(xn, w3_ref[...])
            gated = (_silu(a) * b).astype(BF16)
            for n in range(D_MODEL // FFN_NC):
                cols = slice(n * FFN_NC, (n + 1) * FFN_NC)
                part = _dot(gated, w2_ref[:, cols])
                if first:
                    o_ref[rows, cols] = part
                else:
                    o_ref[rows, cols] += part

    pl.when(j == 0)(functools.partial(step, True))
    pl.when(j > 0)(functools.partial(step, False))

    @pl.when(j == pl.num_programs(1) - 1)
    def _():
        half_gate = 0.5 * mod_ref[0, k0 + 2:k0 + 3, :]
        if final:
            for r in range(0, FFN_TM, ROW_CHUNK):
                rows = slice(r, r + ROW_CHUNK)
                o_ref[rows, :] = _rms(x_ref[rows, :] + half_gate * o_ref[rows, :], gf_ref[...])
        else:
            o_ref[...] = x_ref[...] + half_gate * o_ref[...]


def _cast_kernel(w_ref, o_ref):
    o_ref[...] = w_ref[...].astype(BF16)


def _cast_bf16(w):
    depth, rows, cols = w.shape
    tr = 1 << ((CAST_BLOCK_ELEMS // cols).bit_length() - 1)
    assert rows % tr == 0
    spec = pl.BlockSpec((None, tr, cols), lambda l, i: (l, i, 0))
    return pl.pallas_call(
        _cast_kernel,
        grid=(depth, rows // tr),
        in_specs=[spec],
        out_specs=spec,
        out_shape=jax.ShapeDtypeStruct(w.shape, BF16),
        compiler_params=_params(("parallel", "parallel")),
        name="cast_bf16",
    )(w)


def _ffn(x, mod, row_of_tile, g, w1, w3, w2, l, k0, final_g=None):
    t = x.shape[0]
    final = final_g is not None
    gf = final_g if final else g
    return pl.pallas_call(
        functools.partial(_ffn_kernel, k0=k0, final=final),
        grid=(t // FFN_TM, D_FF // TF),
        in_specs=[
            pl.BlockSpec((FFN_TM, D_MODEL), lambda i, j: (i, 0)),
            _mod_spec(row_of_tile, FFN_TM),
            pl.BlockSpec((1, D_MODEL), lambda i, j: (0, 0)),
            pl.BlockSpec((None, D_MODEL, TF), lambda i, j: (l, 0, j)),
            pl.BlockSpec((None, D_MODEL, TF), lambda i, j: (l, 0, j)),
            pl.BlockSpec((None, TF, D_MODEL), lambda i, j: (l, j, 0)),
            pl.BlockSpec((1, D_MODEL), lambda i, j: (0, 0)),
        ],
        out_specs=pl.BlockSpec((FFN_TM, D_MODEL), lambda i, j: (i, 0)),
        out_shape=jax.ShapeDtypeStruct((t, D_MODEL), F32),
        scratch_shapes=[pltpu.VMEM((FFN_TM, D_MODEL), BF16)],
        compiler_params=_params(("parallel", "arbitrary")),
        name="ffn",
    )(x, mod, g.reshape(1, D_MODEL), w1, w3, w2, gf.reshape(1, D_MODEL))


def _mixer_in_kernel(x_ref, mod_ref, g_ref, wp1_ref, wp2_ref, gq_ref, gkv_ref, wa_ref, wb_ref, wk_ref, wv_ref,
                     cos_ref, sin_ref, q_ref, k_ref, v_ref, p2_ref, xn_ref, *, k0):
    _norm_modulate_rows(x_ref, g_ref, mod_ref, k0, xn_ref)
    xn = xn_ref[...]

    for n in range(3):
        cols = slice(n * D_NA, (n + 1) * D_NA)
        y2 = _dot(xn, wp2_ref[:, cols])
        p2_ref[:, cols] = (y2 * NA_QSCALE if n == 0 else y2).astype(BF16)

    p1 = _dot(xn, wp1_ref[...])
    cos, sin = cos_ref[...], sin_ref[...]

    cqn = _rms(p1[:, :Q_LORA], gq_ref[...]).astype(BF16)
    ya = _dot(cqn, wa_ref[...])
    yb = _dot(cqn, wb_ref[...])
    for h in range(MLA_HEADS):
        lo = h * QK_PAD
        q_ref[:, lo:lo + LANES] = (ya[:, lo:lo + LANES] * MLA_QSCALE).astype(BF16)
        rope = ya[:, lo + LANES:lo + QK_PAD] * cos + yb[:, h * LANES:(h + 1) * LANES] * sin
        q_ref[:, lo + LANES:lo + QK_PAD] = (rope * MLA_QSCALE).astype(BF16)

    ckvn = _rms(p1[:, Q_LORA:Q_LORA + KV_LORA], gkv_ref[...]).astype(BF16)
    yk = _dot(ckvn, wk_ref[...])
    v_ref[...] = _dot(ckvn, wv_ref[...]).astype(BF16)
    kr_lo = Q_LORA + KV_LORA
    kr = (p1[:, kr_lo:kr_lo + LANES] * cos + p1[:, kr_lo + LANES:kr_lo + 2 * LANES] * sin).astype(BF16)
    for h in range(MLA_HEADS):
        lo = h * QK_PAD
        k_ref[:, lo:lo + LANES] = yk[:, h * LANES:(h + 1) * LANES].astype(BF16)
        k_ref[:, lo + LANES:lo + QK_PAD] = kr


def _mixer_in(x, mod, row_of_tile, table_block, g, w_p1, w_p2, gq, gkv, wa, wb, wk, wv, cos_t, sin_t, k0):
    t = x.shape[0]
    tm = MIX_TM

    def resident(shape):
        return pl.BlockSpec(shape, lambda i: (0,) * len(shape), pipeline_mode=pl.Buffered(1))

    return pl.pallas_call(
        functools.partial(_mixer_in_kernel, k0=k0),
        grid=(t // tm,),
        in_specs=[
            pl.BlockSpec((tm, D_MODEL), lambda i: (i, 0)),
            _mod_spec(row_of_tile, tm),
            resident((1, D_MODEL)),
            resident((D_MODEL, P1_COLS)),
            resident((D_MODEL, 3 * D_NA)),
            resident((1, Q_LORA)),
            resident((1, KV_LORA)),
            resident((Q_LORA, MLA_HEADS * QK_PAD)),
            resident((Q_LORA, MLA_HEADS * LANES)),
            resident((KV_LORA, MLA_HEADS * QK_NOPE)),
            resident((KV_LORA, D_MLA_OUT)),
            pl.BlockSpec((tm, LANES), lambda i: (table_block(i, tm), 0)),
            pl.BlockSpec((tm, LANES), lambda i: (table_block(i, tm), 0)),
        ],
        out_specs=[
            pl.BlockSpec((tm, MLA_HEADS * QK_PAD), lambda i: (i, 0)),
            pl.BlockSpec((tm, MLA_HEADS * QK_PAD), lambda i: (i, 0)),
            pl.BlockSpec((tm, D_MLA_OUT), lambda i: (i, 0)),
            pl.BlockSpec((tm, 3 * D_NA), lambda i: (i, 0)),
        ],
        out_shape=[
            jax.ShapeDtypeStruct((t, MLA_HEADS * QK_PAD), BF16),
            jax.ShapeDtypeStruct((t, MLA_HEADS * QK_PAD), BF16),
            jax.ShapeDtypeStruct((t, D_MLA_OUT), BF16),
            jax.ShapeDtypeStruct((t, 3 * D_NA), BF16),
        ],
        scratch_shapes=[pltpu.VMEM((tm, D_MODEL), BF16)],
        compiler_params=_params(("parallel",)),
        name="mixer_in",
    )(x, mod, g.reshape(1, D_MODEL), w_p1, w_p2, gq.reshape(1, Q_LORA), gkv.reshape(1, KV_LORA), wa, wb, wk, wv,
      cos_t, sin_t)


def _softmax_pv(s_parts, v_parts):
    m = functools.reduce(jnp.maximum, [jnp.max(s, axis=-1, keepdims=True) for s in s_parts])
    p_parts = [jnp.exp2(s - m) for s in s_parts]
    l = functools.reduce(jnp.add, [jnp.sum(p, axis=-1, keepdims=True) for p in p_parts])
    o = functools.reduce(jnp.add, [_dot(p.astype(BF16), v) for p, v in zip(p_parts, v_parts)])
    return o * (1.0 / l)


def _mla_kernel(q_ref, kl_ref, kc_ref, vl_ref, vc_ref, o_ref):
    def scores(c):
        q = q_ref[c * MLA_QC:(c + 1) * MLA_QC, :]
        return [_dot_nt(q, kl_ref[...]), _dot_nt(q, kc_ref[...])]

    n_chunks = TQ // MLA_QC
    s = scores(0)
    for c in range(n_chunks):
        s_next = scores(c + 1) if c + 1 < n_chunks else None
        o_ref[c * MLA_QC:(c + 1) * MLA_QC, :] = _softmax_pv(s, [vl_ref[...], vc_ref[...]]).astype(o_ref.dtype)
        s = s_next


def _mla_latent(q, k_lat, k_ctx, v_lat, v_ctx):
    nq = SEQ // TQ
    return pl.pallas_call(
        _mla_kernel,
        grid=(BATCH, MLA_HEADS, nq),
        in_specs=[
            pl.BlockSpec((TQ, QK_PAD), lambda b, h, i: (b * nq + i, h)),
            pl.BlockSpec((SEQ, QK_PAD), lambda b, h, i: (b, h)),
            pl.BlockSpec((CTX_LEN, QK_PAD), lambda b, h, i: (b, h)),
            pl.BlockSpec((SEQ, V_HEAD), lambda b, h, i: (b, h)),
            pl.BlockSpec((CTX_LEN, V_HEAD), lambda b, h, i: (b, h)),
        ],
        out_specs=pl.BlockSpec((TQ, V_HEAD), lambda b, h, i: (b * nq + i, h)),
        out_shape=jax.ShapeDtypeStruct((BATCH * SEQ, D_MLA_OUT), BF16),
        compiler_params=_params(("parallel", "parallel", "arbitrary")),
        name="mla_latent",
    )(q, k_lat, k_ctx, v_lat, v_ctx)


def _ctx_attn_kernel(q_ref, k_ref, v_ref, o_ref, *, width):
    for h in range(MLA_HEADS):
        qk = slice(h * width, (h + 1) * width)
        vo = slice(h * V_HEAD, (h + 1) * V_HEAD)
        o_ref[:, vo] = _softmax_pv([_dot_nt(q_ref[:, qk], k_ref[:, qk])], [v_ref[:, vo]]).astype(o_ref.dtype)


def _ctx_attention(q, k, v, width, q_off, k_off, v_off):
    return pl.pallas_call(
        functools.partial(_ctx_attn_kernel, width=width),
        grid=(BATCH,),
        in_specs=[
            pl.BlockSpec((CTX_LEN, MLA_HEADS * width), lambda b: (b, q_off)),
            pl.BlockSpec((CTX_LEN, MLA_HEADS * width), lambda b: (b, k_off)),
            pl.BlockSpec((CTX_LEN, MLA_HEADS * V_HEAD), lambda b: (b, v_off)),
        ],
        out_specs=pl.BlockSpec((CTX_LEN, MLA_HEADS * V_HEAD), lambda b: (b, 0)),
        out_shape=jax.ShapeDtypeStruct((BATCH * CTX_LEN, MLA_HEADS * V_HEAD), BF16),
        compiler_params=_params(("parallel",)),
        name="ctx_attention",
    )(q, k, v)


def _na_group_window(g):
    if g == 0:
        return 0, 0
    if g == 1:
        return 1, 0
    if g == N_NA_GROUPS - 1:
        return 3, GRID_H - NA_WIN_ROWS
    return 2, NA_GROUP_ROWS * g - WIN_H // 2


def _na_kernel(q_ref, k_ref, v_ref, kc_ref, vc_ref, tab_ref, o_ref):
    gq = NA_GROUP_ROWS * GRID_W
    gk = NA_WIN_ROWS * GRID_W

    def scores(g):
        var, ks = _na_group_window(g)
        q = q_ref[g * gq:(g + 1) * gq, :]
        kw = k_ref[ks * GRID_W:ks * GRID_W + gk, :]
        return [_dot_nt(q, kw) + tab_ref[var], _dot_nt(q, kc_ref[...])]

    s = scores(0)
    for g in range(N_NA_GROUPS):
        s_next = scores(g + 1) if g + 1 < N_NA_GROUPS else None
        _, ks = _na_group_window(g)
        vw = v_ref[ks * GRID_W:ks * GRID_W + gk, :]
        o_ref[g * gq:(g + 1) * gq, :] = _softmax_pv(s, [vw, vc_ref[...]]).astype(o_ref.dtype)
        s = s_next


def _na_tables(rpb):
    nh, nr, nc = rpb.shape
    v = jnp.pad(rpb.astype(F32), ((0, 0), (0, 0), (GRID_W - WIN_W, GRID_W + WIN_W - nc)))
    flat = jnp.broadcast_to(v[:, :, None, :], (nh, nr, GRID_W, 2 * GRID_W)).reshape(nh, nr, 2 * GRID_W * GRID_W)
    toe = flat[:, :, :GRID_W * (2 * GRID_W - 1)].reshape(nh, nr, GRID_W, 2 * GRID_W - 1)[..., GRID_W - 1:]
    row_pad = WIN_H // 2
    toe = jnp.pad(toe, ((0, 0), (row_pad, row_pad), (0, 0), (0, 0)))

    qa = np.arange(NA_GROUP_ROWS).reshape(-1, 1, 1, 1)
    qc = np.arange(GRID_W).reshape(1, -1, 1, 1)
    ki = np.arange(NA_WIN_ROWS).reshape(1, 1, -1, 1)
    kc = np.arange(GRID_W).reshape(1, 1, 1, -1)
    col_start = np.clip(qc - WIN_W // 2, 0, GRID_W - WIN_W)
    col_ok = (kc >= col_start) & (kc < col_start + WIN_W)
    tables = []
    for g in (0, 1, 2, N_NA_GROUPS - 1):
        _, ks = _na_group_window(g)
        r = NA_GROUP_ROWS * g + qa
        rs = np.clip(r - WIN_H // 2, 0, GRID_H - WIN_H)
        krow = ks + ki
        ok = np.broadcast_to((krow >= rs) & (krow < rs + WIN_H) & col_ok,
                             (NA_GROUP_ROWS, GRID_W, NA_WIN_ROWS, GRID_W))
        per_row = []
        for a in range(NA_GROUP_ROWS):
            base = ks - (NA_GROUP_ROWS * g + a) + WIN_H - 1 + row_pad
            assert 0 <= base and base + NA_WIN_ROWS <= nr + 2 * row_pad
            per_row.append(toe[:, base:base + NA_WIN_ROWS])
        bias = jnp.stack(per_row, axis=1).transpose(0, 1, 3, 2, 4)
        bias = jnp.where(ok[None], bias * LOG2E, NEG)
        tables.append(bias.reshape(nh, NA_GROUP_ROWS * GRID_W, NA_WIN_ROWS * GRID_W))
    return jnp.stack(tables)


def _na_latent(p2_lat, p2_ctx, tables):
    nh = NA_HEADS
    gq = NA_GROUP_ROWS * GRID_W
    gk = NA_WIN_ROWS * GRID_W
    return pl.pallas_call(
        _na_kernel,
        grid=(nh, BATCH),
        in_specs=[
            pl.BlockSpec((SEQ, NA_HEAD_DIM), lambda h, b: (b, h)),
            pl.BlockSpec((SEQ, NA_HEAD_DIM), lambda h, b: (b, nh + h)),
            pl.BlockSpec((SEQ, NA_HEAD_DIM), lambda h, b: (b, 2 * nh + h)),
            pl.BlockSpec((CTX_LEN, NA_HEAD_DIM), lambda h, b: (b, nh + h)),
            pl.BlockSpec((CTX_LEN, NA_HEAD_DIM), lambda h, b: (b, 2 * nh + h)),
            pl.BlockSpec((4, None, gq, gk), lambda h, b: (0, h, 0, 0)),
        ],
        out_specs=pl.BlockSpec((SEQ, NA_HEAD_DIM), lambda h, b: (b, h)),
        out_shape=jax.ShapeDtypeStruct((BATCH * SEQ, D_NA), BF16),
        compiler_params=_params(("parallel", "parallel")),
        name="na_latent",
    )(p2_lat, p2_lat, p2_lat, p2_ctx, p2_ctx, tables)


def _merge_kernel(a_ref, b_ref, h_ref, mod_ref, ga_ref, gb_ref, wa_ref, wb_ref, o_ref, *, k0):
    an = _rms(a_ref[...].astype(F32), ga_ref[...]).astype(BF16)
    bn = _rms(b_ref[...].astype(F32), gb_ref[...]).astype(BF16)
    y = _dot(an, wa_ref[...]) + _dot(bn, wb_ref[...])
    o_ref[...] = h_ref[...] + mod_ref[0, k0:k0 + 1, :] * y


def _merge(a, b, h, mod, row_of_tile, out_norm, w_out, k0):
    t = h.shape[0]
    g2 = out_norm.reshape(2, 1, D_MLA_OUT)
    return pl.pallas_call(
        functools.partial(_merge_kernel, k0=k0),
        grid=(t // TM,),
        in_specs=[
            pl.BlockSpec((TM, D_MLA_OUT), lambda i: (i, 0)),
            pl.BlockSpec((TM, D_NA), lambda i: (i, 0)),
            pl.BlockSpec((TM, D_MODEL), lambda i: (i, 0)),
            _mod_spec(row_of_tile, TM),
            pl.BlockSpec((None, 1, D_MLA_OUT), lambda i: (0, 0, 0)),
            pl.BlockSpec((None, 1, D_NA), lambda i: (1, 0, 0)),
            pl.BlockSpec((D_MLA_OUT, D_MODEL), lambda i: (0, 0)),
            pl.BlockSpec((D_NA, D_MODEL), lambda i: (1, 0)),
        ],
        out_specs=pl.BlockSpec((TM, D_MODEL), lambda i: (i, 0)),
        out_shape=jax.ShapeDtypeStruct((t, D_MODEL), F32),
        compiler_params=_params(("parallel",)),
        name="merge",
    )(a, b, h, mod, g2, g2, w_out, w_out)


def _rope_tables():
    t = np.arange(SEQ)
    pos = np.stack([t // GRID_W, t % GRID_W], axis=-1).astype(np.float32)
    inv_freq = jnp.asarray(ROPE_THETA, F32) ** (-jnp.arange(N_FREQ, dtype=F32) / N_FREQ)
    ang = jnp.asarray(pos)[:, :, None] * inv_freq
    cos, sin = jnp.cos(ang), jnp.sin(ang)
    cos_t = jnp.concatenate([cos, cos], axis=-1).reshape(SEQ, QK_ROPE)
    sin_t = jnp.concatenate([-sin, sin], axis=-1).reshape(SEQ, QK_ROPE)
    pad = jnp.zeros((SEQ, LANES - QK_ROPE), F32)
    cos_t = jnp.concatenate([cos_t, pad], axis=-1)
    sin_t = jnp.concatenate([sin_t, pad], axis=-1)
    ident = jnp.concatenate([jnp.ones((TM, QK_ROPE), F32), jnp.zeros((TM, LANES - QK_ROPE), F32)], axis=-1)
    return (jnp.concatenate([cos_t, ident], axis=0),
            jnp.concatenate([sin_t, jnp.zeros((TM, LANES), F32)], axis=0))


def _rotate_half_cols(w):
    blocks = w.reshape(w.shape[:-1] + (2, 2, N_FREQ))
    return blocks[..., ::-1, :].reshape(w.shape)


def _layer_weights(w_in, w_uq, w_ukv):
    kr = w_in[:, 2 * Q_LORA:2 * Q_LORA + QK_ROPE]
    zpad = jnp.zeros((D_MODEL, LANES - QK_ROPE), F32)
    w_p1 = jnp.concatenate([w_in[:, :2 * Q_LORA], kr, zpad, _rotate_half_cols(kr), zpad], axis=1).astype(BF16)
    w_p2 = w_in[:, 2 * Q_LORA + QK_ROPE:].astype(BF16)
    uq = w_uq.reshape(Q_LORA, MLA_HEADS, QK_NOPE + QK_ROPE)
    hz = jnp.zeros((Q_LORA, MLA_HEADS, QK_PAD - QK_NOPE - QK_ROPE), F32)
    wa = jnp.concatenate([uq, hz], axis=-1).reshape(Q_LORA, MLA_HEADS * QK_PAD).astype(BF16)
    wb = jnp.concatenate([_rotate_half_cols(uq[..., QK_NOPE:]), hz], axis=-1)
    wb = wb.reshape(Q_LORA, MLA_HEADS * LANES).astype(BF16)
    ukv = w_ukv.reshape(KV_LORA, MLA_HEADS, QK_NOPE + V_HEAD)
    wk = ukv[..., :QK_NOPE].reshape(KV_LORA, MLA_HEADS * QK_NOPE).astype(BF16)
    wv = ukv[..., QK_NOPE:].reshape(KV_LORA, D_MLA_OUT).astype(BF16)
    return w_p1, w_p2, wa, wb, wk, wv


def kernel(x, c, ctx, c_ctx, ada_w, ada_b, ffn1_norm, ffn1_w1, ffn1_w3, ffn1_w2, mix_norm, w_in, qa_norm, w_uq,
           kva_norm, w_ukv, na_rpb, out_norm, w_out, ffn2_norm, ffn2_w1, ffn2_w3, ffn2_w2, final_norm):
    c_rows = jnp.concatenate([c, c_ctx[None, :], jnp.zeros((MOD_ROWS - BATCH - 1, D_MODEL), F32)], axis=0)
    mod = _modulation(c_rows, ada_w, ada_b)
    cos_t, sin_t = _rope_tables()
    lat_tab = lambda i, tm: i % (SEQ // tm)
    ctx_tab = lambda i, tm: SEQ // tm
    ffn1_w = tuple(_cast_bf16(w) for w in (ffn1_w1, ffn1_w3, ffn1_w2))
    ffn2_w = tuple(_cast_bf16(w) for w in (ffn2_w1, ffn2_w3, ffn2_w2))

    h = x.reshape(BATCH * SEQ, D_MODEL)
    hc = ctx.reshape(BATCH * CTX_LEN, D_MODEL)
    for l in range(DEPTH):
        last = l == DEPTH - 1
        lat_row = lambda i, tm, l=l: l * MOD_ROWS + (i * tm) // SEQ
        ctx_row = lambda i, tm, l=l: l * MOD_ROWS + CTX_MOD_ROW
        h = _ffn(h, mod, lat_row, ffn1_norm[l], *ffn1_w, l, 0)
        hc = _ffn(hc, mod, ctx_row, ffn1_norm[l], *ffn1_w, l, 0)

        w_p1, w_p2, wa, wb, wk, wv = _layer_weights(w_in[l], w_uq[l], w_ukv[l])
        tables = _na_tables(na_rpb[l])
        mix_w = (mix_norm[l], w_p1, w_p2, qa_norm[l], kva_norm[l], wa, wb, wk, wv, cos_t, sin_t, 3)
        q, k, v, p2 = _mixer_in(h, mod, lat_row, lat_tab, *mix_w)
        qc, kc, vc, p2c = _mixer_in(hc, mod, ctx_row, ctx_tab, *mix_w)
        a_lat = _mla_latent(q, k, kc, v, vc)
        b_lat = _na_latent(p2, p2c, tables)
        h = _merge(a_lat, b_lat, h, mod, lat_row, out_norm[l], w_out[l].astype(BF16), 5)
        h = _ffn(h, mod, lat_row, ffn2_norm[l], *ffn2_w, l, 6, final_g=final_norm if last else None)
        if not last:
            a_ctx = _ctx_attention(qc, kc, vc, QK_PAD, 0, 0, 0)
            b_ctx = _ctx_attention(p2c, p2c, p2c, NA_HEAD_DIM, 0, 1, 2)
            hc = _merge(a_ctx, b_ctx, hc, mod, ctx_row, out_norm[l], w_out[l].astype(BF16), 5)
            hc = _ffn(hc, mod, ctx_row, ffn2_norm[l], *ffn2_w, l, 6)
    return h.reshape(BATCH, SEQ, D_MODEL)
```

```python
import functools

import numpy as np
import jax
import jax.numpy as jnp
from jax import lax
from jax.experimental import pallas as pl
from jax.experimental.pallas import tpu as pltpu

D_MODEL = 2048
BATCH = 16
SEQ = 2048
DEPTH = 2
GRID_W = 64
GRID_H = SEQ // GRID_W
CTX_LEN = 256
MLA_HEADS = 8
QK_NOPE = 128
QK_ROPE = 64
V_HEAD = 128
Q_LORA = 512
KV_LORA = 512
N_FREQ = QK_ROPE // 4
ROPE_THETA = 10000.0
MLA_SCALE = (QK_NOPE + QK_ROPE) ** -0.5
NA_HEADS = 8
NA_HEAD_DIM = 128
WIN_H = 8
WIN_W = 16
NA_SCALE = NA_HEAD_DIM ** -0.5
LOG2E = 1.4426950408889634
MLA_QSCALE = MLA_SCALE * LOG2E
NA_QSCALE = NA_SCALE * LOG2E
D_MLA_OUT = MLA_HEADS * V_HEAD
D_NA = NA_HEADS * NA_HEAD_DIM
D_FF = 5632
N_MOD = 9
EPS = 1e-6
NEG = -1e30

F32 = jnp.float32
BF16 = jnp.bfloat16

LANES = 128
V7X_VMEM_LIMIT = 60000 * 1024

MOD_ROWS = 24
CTX_MOD_ROW = BATCH
QK_PAD = 256
P1_COLS = 2 * Q_LORA + 2 * LANES
NA_GROUP_ROWS = 4
NA_WIN_ROWS = 12
N_NA_GROUPS = GRID_H // NA_GROUP_ROWS

CAST_BLOCK_ELEMS = 1024 * 1024
ROW_CHUNK = 16
MIX_TM = 256
TM = 512
FFN_TM = 1024
FFN_MC = 512
FFN_NC = 512
TF = 512
TQ = 2048
MLA_QC = 512


def _params(sem, vmem=V7X_VMEM_LIMIT):
    return pltpu.CompilerParams(dimension_semantics=sem, vmem_limit_bytes=vmem)


def _rms(x, g):
    return x * lax.rsqrt(jnp.mean(x * x, axis=-1, keepdims=True) + EPS) * g


def _silu(a):
    return a * (1.0 / (1.0 + jnp.exp(-a)))


def _dot(a, b):
    return jnp.dot(a, b, preferred_element_type=F32)


def _dot_nt(a, b):
    return lax.dot_general(a, b, (((1,), (1,)), ((), ())), preferred_element_type=F32)


def _mod_kernel(s_ref, w_ref, b_ref, o_ref):
    s = _silu(s_ref[...])
    o_ref[...] = _dot(s.astype(BF16), w_ref[...].astype(BF16)) + b_ref[...]


def _modulation(c_rows, ada_w, ada_b):
    tn = 1024
    n = N_MOD * D_MODEL
    out = pl.pallas_call(
        _mod_kernel,
        grid=(DEPTH, n // tn),
        in_specs=[
            pl.BlockSpec((MOD_ROWS, D_MODEL), lambda l, j: (0, 0)),
            pl.BlockSpec((None, D_MODEL, tn), lambda l, j: (l, 0, j)),
            pl.BlockSpec((None, 1, tn), lambda l, j: (l, 0, j)),
        ],
        out_specs=pl.BlockSpec((None, MOD_ROWS, tn), lambda l, j: (l, 0, j)),
        out_shape=jax.ShapeDtypeStruct((DEPTH, MOD_ROWS, n), F32),
        compiler_params=_params(("parallel", "parallel")),
        name="adaln_mod",
    )(c_rows, ada_w, ada_b.reshape(DEPTH, 1, n))
    return out.reshape(DEPTH * MOD_ROWS, N_MOD, D_MODEL)


def _norm_modulate_rows(x_ref, g_ref, mod_ref, k0, xn_ref):
    gain = g_ref[...] * (1.0 + mod_ref[0, k0 + 1:k0 + 2, :])
    shift = mod_ref[0, k0:k0 + 1, :]

    def body(r, carry):
        rows = pl.ds(pl.multiple_of(r * ROW_CHUNK, ROW_CHUNK), ROW_CHUNK)
        x = x_ref[rows, :]
        y = x * lax.rsqrt(jnp.mean(x * x, axis=-1, keepdims=True) + EPS)
        xn_ref[rows, :] = (y * gain + shift).astype(BF16)
        return carry

    lax.fori_loop(0, x_ref.shape[0] // ROW_CHUNK, body, 0, unroll=8)


def _mod_spec(row_of_tile, tm):
    return pl.BlockSpec((1, N_MOD, D_MODEL), lambda i, *_: (row_of_tile(i, tm), 0, 0))


def _ffn_kernel(x_ref, mod_ref, g_ref, w1_ref, w3_ref, w2_ref, gf_ref, o_ref, xn_ref, *, k0, final):
    j = pl.program_id(1)

    def step(first):
        if first:
            gain = g_ref[...] * (1.0 + mod_ref[0, k0 + 1:k0 + 2, :])
            shift = mod_ref[0, k0:k0 + 1, :]
        for m in range(FFN_TM // FFN_MC):
            if first:
                for r in range(m * FFN_MC, (m + 1) * FFN_MC, ROW_CHUNK):
                    x = x_ref[r:r + ROW_CHUNK, :]
                    y = x * lax.rsqrt(jnp.mean(x * x, axis=-1, keepdims=True) + EPS)
                    xn_ref[r:r + ROW_CHUNK, :] = (y * gain + shift).astype(BF16)
            rows = slice(m * FFN_MC, (m + 1) * FFN_MC)
            xn = xn_ref[rows, :]
            a = _dot(xn, w1_ref[...])
            b = _dot(xn, w3_ref[...])
            gated = (_silu(a) * b).astype(BF16)
            for n in range(D_MODEL // FFN_NC):
                cols = slice(n * FFN_NC, (n + 1) * FFN_NC)
                part = _dot(gated, w2_ref[:, cols])
                if first:
                    o_ref[rows, cols] = part
                else:
                    o_ref[rows, cols] += part

    pl.when(j == 0)(functools.partial(step, True))
    pl.when(j > 0)(functools.partial(step, False))

    @pl.when(j == pl.num_programs(1) - 1)
    def _():
        half_gate = 0.5 * mod_ref[0, k0 + 2:k0 + 3, :]
        if final:
            for r in range(0, FFN_TM, ROW_CHUNK):
                rows = slice(r, r + ROW_CHUNK)
                o_ref[rows, :] = _rms(x_ref[rows, :] + half_gate * o_ref[rows, :], gf_ref[...])
        else:
            o_ref[...] = x_ref[...] + half_gate * o_ref[...]


def _cast_kernel(w_ref, o_ref):
    o_ref[...] = w_ref[...].astype(BF16)


def _cast_bf16(w):
    depth, rows, cols = w.shape
    tr = 1 << ((CAST_BLOCK_ELEMS // cols).bit_length() - 1)
    assert rows % tr == 0
    spec = pl.BlockSpec((None, tr, cols), lambda l, i: (l, i, 0))
    return pl.pallas_call(
        _cast_kernel,
        grid=(depth, rows // tr),
        in_specs=[spec],
        out_specs=spec,
        out_shape=jax.ShapeDtypeStruct(w.shape, BF16),
        compiler_params=_params(("parallel", "parallel")),
        name="cast_bf16",
    )(w)


def _ffn(x, mod, row_of_tile, g, w1, w3, w2, l, k0, final_g=None):
    t = x.shape[0]
    final = final_g is not None
    gf = final_g if final else g
    return pl.pallas_call(
        functools.partial(_ffn_kernel, k0=k0, final=final),
        grid=(t // FFN_TM, D_FF // TF),
        in_specs=[
            pl.BlockSpec((FFN_TM, D_MODEL), lambda i, j: (i, 0)),
            _mod_spec(row_of_tile, FFN_TM),
            pl.BlockSpec((1, D_MODEL), lambda i, j: (0, 0)),
            pl.BlockSpec((None, D_MODEL, TF), lambda i, j: (l, 0, j)),
            pl.BlockSpec((None, D_MODEL, TF), lambda i, j: (l, 0, j)),
            pl.BlockSpec((None, TF, D_MODEL), lambda i, j: (l, j, 0)),
            pl.BlockSpec((1, D_MODEL), lambda i, j: (0, 0)),
        ],
        out_specs=pl.BlockSpec((FFN_TM, D_MODEL), lambda i, j: (i, 0)),
        out_shape=jax.ShapeDtypeStruct((t, D_MODEL), F32),
        scratch_shapes=[pltpu.VMEM((FFN_TM, D_MODEL), BF16)],
        compiler_params=_params(("parallel", "arbitrary")),
        name="ffn",
    )(x, mod, g.reshape(1, D_MODEL), w1, w3, w2, gf.reshape(1, D_MODEL))


def _mixer_in_kernel(x_ref, mod_ref, g_ref, wp1_ref, wp2_ref, gq_ref, gkv_ref, wa_ref, wb_ref, wk_ref, wv_ref,
                     cos_ref, sin_ref, q_ref, k_ref, v_ref, p2_ref, xn_ref, *, k0):
    _norm_modulate_rows(x_ref, g_ref, mod_ref, k0, xn_ref)
    xn = xn_ref[...]

    for n in range(3):
        cols = slice(n * D_NA, (n + 1) * D_NA)
        y2 = _dot(xn, wp2_ref[:, cols])
        p2_ref[:, cols] = (y2 * NA_QSCALE if n == 0 else y2).astype(BF16)

    p1 = _dot(xn, wp1_ref[...])
    cos, sin = cos_ref[...], sin_ref[...]

    cqn = _rms(p1[:, :Q_LORA], gq_ref[...]).astype(BF16)
    ya = _dot(cqn, wa_ref[...])
    yb = _dot(cqn, wb_ref[...])
    for h in range(MLA_HEADS):
        lo = h * QK_PAD
        q_ref[:, lo:lo + LANES] = (ya[:, lo:lo + LANES] * MLA_QSCALE).astype(BF16)
        rope = ya[:, lo + LANES:lo + QK_PAD] * cos + yb[:, h * LANES:(h + 1) * LANES] * sin
        q_ref[:, lo + LANES:lo + QK_PAD] = (rope * MLA_QSCALE).astype(BF16)

    ckvn = _rms(p1[:, Q_LORA:Q_LORA + KV_LORA], gkv_ref[...]).astype(BF16)
    yk = _dot(ckvn, wk_ref[...])
    v_ref[...] = _dot(ckvn, wv_ref[...]).astype(BF16)
    kr_lo = Q_LORA + KV_LORA
    kr = (p1[:, kr_lo:kr_lo + LANES] * cos + p1[:, kr_lo + LANES:kr_lo + 2 * LANES] * sin).astype(BF16)
    for h in range(MLA_HEADS):
        lo = h * QK_PAD
        k_ref[:, lo:lo + LANES] = yk[:, h * LANES:(h + 1) * LANES].astype(BF16)
        k_ref[:, lo + LANES:lo + QK_PAD] = kr


def _mixer_in(x, mod, row_of_tile, table_block, g, w_p1, w_p2, gq, gkv, wa, wb, wk, wv, cos_t, sin_t, k0):
    t = x.shape[0]
    tm = MIX_TM

    def resident(shape):
        return pl.BlockSpec(shape, lambda i: (0,) * len(shape), pipeline_mode=pl.Buffered(1))

    return pl.pallas_call(
        functools.partial(_mixer_in_kernel, k0=k0),
        grid=(t // tm,),
        in_specs=[
            pl.BlockSpec((tm, D_MODEL), lambda i: (i, 0)),
            _mod_spec(row_of_tile, tm),
            resident((1, D_MODEL)),
            resident((D_MODEL, P1_COLS)),
            resident((D_MODEL, 3 * D_NA)),
            resident((1, Q_LORA)),
            resident((1, KV_LORA)),
            resident((Q_LORA, MLA_HEADS * QK_PAD)),
            resident((Q_LORA, MLA_HEADS * LANES)),
            resident((KV_LORA, MLA_HEADS * QK_NOPE)),
            resident((KV_LORA, D_MLA_OUT)),
            pl.BlockSpec((tm, LANES), lambda i: (table_block(i, tm), 0)),
            pl.BlockSpec((tm, LANES), lambda i: (table_block(i, tm), 0)),
        ],
        out_specs=[
            pl.BlockSpec((tm, MLA_HEADS * QK_PAD), lambda i: (i, 0)),
            pl.BlockSpec((tm, MLA_HEADS * QK_PAD), lambda i: (i, 0)),
            pl.BlockSpec((tm, D_MLA_OUT), lambda i: (i, 0)),
            pl.BlockSpec((tm, 3 * D_NA), lambda i: (i, 0)),
        ],
        out_shape=[
            jax.ShapeDtypeStruct((t, MLA_HEADS * QK_PAD), BF16),
            jax.ShapeDtypeStruct((t, MLA_HEADS * QK_PAD), BF16),
            jax.ShapeDtypeStruct((t, D_MLA_OUT), BF16),
            jax.ShapeDtypeStruct((t, 3 * D_NA), BF16),
        ],
        scratch_shapes=[pltpu.VMEM((tm, D_MODEL), BF16)],
        compiler_params=_params(("parallel",)),
        name="mixer_in",
    )(x, mod, g.reshape(1, D_MODEL), w_p1, w_p2, gq.reshape(1, Q_LORA), gkv.reshape(1, KV_LORA), wa, wb, wk, wv,
      cos_t, sin_t)


def _softmax_pv(s_parts, v_parts):
    m = functools.reduce(jnp.maximum, [jnp.max(s, axis=-1, keepdims=True) for s in s_parts])
    p_parts = [jnp.exp2(s - m) for s in s_parts]
    l = functools.reduce(jnp.add, [jnp.sum(p, axis=-1, keepdims=True) for p in p_parts])
    o = functools.reduce(jnp.add, [_dot(p.astype(BF16), v) for p, v in zip(p_parts, v_parts)])
    return o * (1.0 / l)


def _mla_kernel(q_ref, kl_ref, kc_ref, vl_ref, vc_ref, o_ref):
    def scores(c):
        q = q_ref[c * MLA_QC:(c + 1) * MLA_QC, :]
        return [_dot_nt(q, kl_ref[...]), _dot_nt(q, kc_ref[...])]

    n_chunks = TQ // MLA_QC
    s = scores(0)
    for c in range(n_chunks):
        s_next = scores(c + 1) if c + 1 < n_chunks else None
        o_ref[c * MLA_QC:(c + 1) * MLA_QC, :] = _softmax_pv(s, [vl_ref[...], vc_ref[...]]).astype(o_ref.dtype)
        s = s_next


def _mla_latent(q, k_lat, k_ctx, v_lat, v_ctx):
    nq = SEQ // TQ
    return pl.pallas_call(
        _mla_kernel,
        grid=(BATCH, MLA_HEADS, nq),
        in_specs=[
            pl.BlockSpec((TQ, QK_PAD), lambda b, h, i: (b * nq + i, h)),
            pl.BlockSpec((SEQ, QK_PAD), lambda b, h, i: (b, h)),
            pl.BlockSpec((CTX_LEN, QK_PAD), lambda b, h, i: (b, h)),
            pl.BlockSpec((SEQ, V_HEAD), lambda b, h, i: (b, h)),
            pl.BlockSpec((CTX_LEN, V_HEAD), lambda b, h, i: (b, h)),
        ],
        out_specs=pl.BlockSpec((TQ, V_HEAD), lambda b, h, i: (b * nq + i, h)),
        out_shape=jax.ShapeDtypeStruct((BATCH * SEQ, D_MLA_OUT), BF16),
        compiler_params=_params(("parallel", "parallel", "arbitrary")),
        name="mla_latent",
    )(q, k_lat, k_ctx, v_lat, v_ctx)


def _ctx_attn_kernel(q_ref, k_ref, v_ref, o_ref, *, width):
    for h in range(MLA_HEADS):
        qk = slice(h * width, (h + 1) * width)
        vo = slice(h * V_HEAD, (h + 1) * V_HEAD)
        o_ref[:, vo] = _softmax_pv([_dot_nt(q_ref[:, qk], k_ref[:, qk])], [v_ref[:, vo]]).astype(o_ref.dtype)


def _ctx_attention(q, k, v, width, q_off, k_off, v_off):
    return pl.pallas_call(
        functools.partial(_ctx_attn_kernel, width=width),
        grid=(BATCH,),
        in_specs=[
            pl.BlockSpec((CTX_LEN, MLA_HEADS * width), lambda b: (b, q_off)),
            pl.BlockSpec((CTX_LEN, MLA_HEADS * width), lambda b: (b, k_off)),
            pl.BlockSpec((CTX_LEN, MLA_HEADS * V_HEAD), lambda b: (b, v_off)),
        ],
        out_specs=pl.BlockSpec((CTX_LEN, MLA_HEADS * V_HEAD), lambda b: (b, 0)),
        out_shape=jax.ShapeDtypeStruct((BATCH * CTX_LEN, MLA_HEADS * V_HEAD), BF16),
        compiler_params=_params(("parallel",)),
        name="ctx_attention",
    )(q, k, v)


def _na_group_window(g):
    if g == 0:
        return 0, 0
    if g == 1:
        return 1, 0
    if g == N_NA_GROUPS - 1:
        return 3, GRID_H - NA_WIN_ROWS
    return 2, NA_GROUP_ROWS * g - WIN_H // 2


def _na_kernel(q_ref, k_ref, v_ref, kc_ref, vc_ref, tab_ref, o_ref):
    gq = NA_GROUP_ROWS * GRID_W
    gk = NA_WIN_ROWS * GRID_W

    def scores(g):
        var, ks = _na_group_window(g)
        q = q_ref[g * gq:(g + 1) * gq, :]
        kw = k_ref[ks * GRID_W:ks * GRID_W + gk, :]
        return [_dot_nt(q, kw) + tab_ref[var], _dot_nt(q, kc_ref[...])]

    s = scores(0)
    for g in range(N_NA_GROUPS):
        s_next = scores(g + 1) if g + 1 < N_NA_GROUPS else None
        _, ks = _na_group_window(g)
        vw = v_ref[ks * GRID_W:ks * GRID_W + gk, :]
        o_ref[g * gq:(g + 1) * gq, :] = _softmax_pv(s, [vw, vc_ref[...]]).astype(o_ref.dtype)
        s = s_next


def _na_tables(rpb):
    nh, nr, nc = rpb.shape
    v = jnp.pad(rpb.astype(F32), ((0, 0), (0, 0), (GRID_W - WIN_W, GRID_W + WIN_W - nc)))
    flat = jnp.broadcast_to(v[:, :, None, :], (nh, nr, GRID_W, 2 * GRID_W)).reshape(nh, nr, 2 * GRID_W * GRID_W)
    toe = flat[:, :, :GRID_W * (2 * GRID_W - 1)].reshape(nh, nr, GRID_W, 2 * GRID_W - 1)[..., GRID_W - 1:]
    row_pad = WIN_H // 2
    toe = jnp.pad(toe, ((0, 0), (row_pad, row_pad), (0, 0), (0, 0)))

    qa = np.arange(NA_GROUP_ROWS).reshape(-1, 1, 1, 1)
    qc = np.arange(GRID_W).reshape(1, -1, 1, 1)
    ki = np.arange(NA_WIN_ROWS).reshape(1, 1, -1, 1)
    kc = np.arange(GRID_W).reshape(1, 1, 1, -1)
    col_start = np.clip(qc - WIN_W // 2, 0, GRID_W - WIN_W)
    col_ok = (kc >= col_start) & (kc < col_start + WIN_W)
    tables = []
    for g in (0, 1, 2, N_NA_GROUPS - 1):
        _, ks = _na_group_window(g)
        r = NA_GROUP_ROWS * g + qa
        rs = np.clip(r - WIN_H // 2, 0, GRID_H - WIN_H)
        krow = ks + ki
        ok = np.broadcast_to((krow >= rs) & (krow < rs + WIN_H) & col_ok,
                             (NA_GROUP_ROWS, GRID_W, NA_WIN_ROWS, GRID_W))
        per_row = []
        for a in range(NA_GROUP_ROWS):
            base = ks - (NA_GROUP_ROWS * g + a) + WIN_H - 1 + row_pad
            assert 0 <= base and base + NA_WIN_ROWS <= nr + 2 * row_pad
            per_row.append(toe[:, base:base + NA_WIN_ROWS])
        bias = jnp.stack(per_row, axis=1).transpose(0, 1, 3, 2, 4)
        bias = jnp.where(ok[None], bias * LOG2E, NEG)
        tables.append(bias.reshape(nh, NA_GROUP_ROWS * GRID_W, NA_WIN_ROWS * GRID_W))
    return jnp.stack(tables)


def _na_latent(p2_lat, p2_ctx, tables, l):
    nh = NA_HEADS
    gq = NA_GROUP_ROWS * GRID_W
    gk = NA_WIN_ROWS * GRID_W
    return pl.pallas_call(
        _na_kernel,
        grid=(nh, BATCH),
        in_specs=[
            pl.BlockSpec((SEQ, NA_HEAD_DIM), lambda h, b: (b, h)),
            pl.BlockSpec((SEQ, NA_HEAD_DIM), lambda h, b: (b, nh + h)),
            pl.BlockSpec((SEQ, NA_HEAD_DIM), lambda h, b: (b, 2 * nh + h)),
            pl.BlockSpec((CTX_LEN, NA_HEAD_DIM), lambda h, b: (b, nh + h)),
            pl.BlockSpec((CTX_LEN, NA_HEAD_DIM), lambda h, b: (b, 2 * nh + h)),
            pl.BlockSpec((4, None, gq, gk), lambda h, b: (0, l * nh + h, 0, 0)),
        ],
        out_specs=pl.BlockSpec((SEQ, NA_HEAD_DIM), lambda h, b: (b, h)),
        out_shape=jax.ShapeDtypeStruct((BATCH * SEQ, D_NA), BF16),
        compiler_params=_params(("parallel", "parallel")),
        name="na_latent",
    )(p2_lat, p2_lat, p2_lat, p2_ctx, p2_ctx, tables)


def _merge_kernel(a_ref, b_ref, h_ref, mod_ref, ga_ref, gb_ref, wa_ref, wb_ref, o_ref, *, k0):
    an = _rms(a_ref[...].astype(F32), ga_ref[...]).astype(BF16)
    bn = _rms(b_ref[...].astype(F32), gb_ref[...]).astype(BF16)
    y = _dot(an, wa_ref[...]) + _dot(bn, wb_ref[...])
    o_ref[...] = h_ref[...] + mod_ref[0, k0:k0 + 1, :] * y


def _merge(a, b, h, mod, row_of_tile, out_norm, w_out, k0):
    t = h.shape[0]
    g2 = out_norm.reshape(2, 1, D_MLA_OUT)
    return pl.pallas_call(
        functools.partial(_merge_kernel, k0=k0),
        grid=(t // TM,),
        in_specs=[
            pl.BlockSpec((TM, D_MLA_OUT), lambda i: (i, 0)),
            pl.BlockSpec((TM, D_NA), lambda i: (i, 0)),
            pl.BlockSpec((TM, D_MODEL), lambda i: (i, 0)),
            _mod_spec(row_of_tile, TM),
            pl.BlockSpec((None, 1, D_MLA_OUT), lambda i: (0, 0, 0)),
            pl.BlockSpec((None, 1, D_NA), lambda i: (1, 0, 0)),
            pl.BlockSpec((D_MLA_OUT, D_MODEL), lambda i: (0, 0)),
            pl.BlockSpec((D_NA, D_MODEL), lambda i: (1, 0)),
        ],
        out_specs=pl.BlockSpec((TM, D_MODEL), lambda i: (i, 0)),
        out_shape=jax.ShapeDtypeStruct((t, D_MODEL), F32),
        compiler_params=_params(("parallel",)),
        name="merge",
    )(a, b, h, mod, g2, g2, w_out, w_out)


def _rope_tables():
    t = np.arange(SEQ)
    pos = np.stack([t // GRID_W, t % GRID_W], axis=-1).astype(np.float32)
    inv_freq = jnp.asarray(ROPE_THETA, F32) ** (-jnp.arange(N_FREQ, dtype=F32) / N_FREQ)
    ang = jnp.asarray(pos)[:, :, None] * inv_freq
    cos, sin = jnp.cos(ang), jnp.sin(ang)
    cos_t = jnp.concatenate([cos, cos], axis=-1).reshape(SEQ, QK_ROPE)
    sin_t = jnp.concatenate([-sin, sin], axis=-1).reshape(SEQ, QK_ROPE)
    pad = jnp.zeros((SEQ, LANES - QK_ROPE), F32)
    cos_t = jnp.concatenate([cos_t, pad], axis=-1)
    sin_t = jnp.concatenate([sin_t, pad], axis=-1)
    ident = jnp.concatenate([jnp.ones((TM, QK_ROPE), F32), jnp.zeros((TM, LANES - QK_ROPE), F32)], axis=-1)
    return (jnp.concatenate([cos_t, ident], axis=0),
            jnp.concatenate([sin_t, jnp.zeros((TM, LANES), F32)], axis=0))


def _rotate_half_cols(w):
    blocks = w.reshape(w.shape[:-1] + (2, 2, N_FREQ))
    return blocks[..., ::-1, :].reshape(w.shape)


def _layer_weights(w_in, w_uq, w_ukv):
    kr = w_in[:, 2 * Q_LORA:2 * Q_LORA + QK_ROPE]
    zpad = jnp.zeros((D_MODEL, LANES - QK_ROPE), F32)
    w_p1 = jnp.concatenate([w_in[:, :2 * Q_LORA], kr, zpad, _rotate_half_cols(kr), zpad], axis=1).astype(BF16)
    w_p2 = w_in[:, 2 * Q_LORA + QK_ROPE:].astype(BF16)
    uq = w_uq.reshape(Q_LORA, MLA_HEADS, QK_NOPE + QK_ROPE)
    hz = jnp.zeros((Q_LORA, MLA_HEADS, QK_PAD - QK_NOPE - QK_ROPE), F32)
    wa = jnp.concatenate([uq, hz], axis=-1).reshape(Q_LORA, MLA_HEADS * QK_PAD).astype(BF16)
    wb = jnp.concatenate([_rotate_half_cols(uq[..., QK_NOPE:]), hz], axis=-1)
    wb = wb.reshape(Q_LORA, MLA_HEADS * LANES).astype(BF16)
    ukv = w_ukv.reshape(KV_LORA, MLA_HEADS, QK_NOPE + V_HEAD)
    wk = ukv[..., :QK_NOPE].reshape(KV_LORA, MLA_HEADS * QK_NOPE).astype(BF16)
    wv = ukv[..., QK_NOPE:].reshape(KV_LORA, D_MLA_OUT).astype(BF16)
    return w_p1, w_p2, wa, wb, wk, wv


def kernel(x, c, ctx, c_ctx, ada_w, ada_b, ffn1_norm, ffn1_w1, ffn1_w3, ffn1_w2, mix_norm, w_in, qa_norm, w_uq,
           kva_norm, w_ukv, na_rpb, out_norm, w_out, ffn2_norm, ffn2_w1, ffn2_w3, ffn2_w2, final_norm):
    c_rows = jnp.concatenate([c, c_ctx[None, :], jnp.zeros((MOD_ROWS - BATCH - 1, D_MODEL), F32)], axis=0)
    mod = _modulation(c_rows, ada_w, ada_b)
    cos_t, sin_t = _rope_tables()
    lat_tab = lambda i, tm: i % (SEQ // tm)
    ctx_tab = lambda i, tm: SEQ // tm
    tables = _na_tables(na_rpb.reshape(DEPTH * NA_HEADS, 2 * WIN_H - 1, 2 * WIN_W - 1))
    ffn1_w = tuple(_cast_bf16(w) for w in (ffn1_w1, ffn1_w3, ffn1_w2))
    ffn2_w = tuple(_cast_bf16(w) for w in (ffn2_w1, ffn2_w3, ffn2_w2))

    h = x.reshape(BATCH * SEQ, D_MODEL)
    hc = ctx.reshape(BATCH * CTX_LEN, D_MODEL)
    for l in range(DEPTH):
        last = l == DEPTH - 1
        lat_row = lambda i, tm, l=l: l * MOD_ROWS + (i * tm) // SEQ
        ctx_row = lambda i, tm, l=l: l * MOD_ROWS + CTX_MOD_ROW
        h = _ffn(h, mod, lat_row, ffn1_norm[l], *ffn1_w, l, 0)
        hc = _ffn(hc, mod, ctx_row, ffn1_norm[l], *ffn1_w, l, 0)

        w_p1, w_p2, wa, wb, wk, wv = _layer_weights(w_in[l], w_uq[l], w_ukv[l])
        mix_w = (mix_norm[l], w_p1, w_p2, qa_norm[l], kva_norm[l], wa, wb, wk, wv, cos_t, sin_t, 3)
        q, k, v, p2 = _mixer_in(h, mod, lat_row, lat_tab, *mix_w)
        qc, kc, vc, p2c = _mixer_in(hc, mod, ctx_row, ctx_tab, *mix_w)
        a_lat = _mla_latent(q, k, kc, v, vc)
        b_lat = _na_latent(p2, p2c, tables, l)
        h = _merge(a_lat, b_lat, h, mod, lat_row, out_norm[l], w_out[l].astype(BF16), 5)
        h = _ffn(h, mod, lat_row, ffn2_norm[l], *ffn2_w, l, 6, final_g=final_norm if last else None)
        if not last:
            a_ctx = _ctx_attention(qc, kc, vc, QK_PAD, 0, 0, 0)
            b_ctx = _ctx_attention(p2c, p2c, p2c, NA_HEAD_DIM, 0, 1, 2)
            hc = _merge(a_ctx, b_ctx, hc, mod, ctx_row, out_norm[l], w_out[l].astype(BF16), 5)
            hc = _ffn(hc, mod, ctx_row, ffn2_norm[l], *ffn2_w, l, 6)
    return h.reshape(BATCH, SEQ, D_MODEL)
```

```python
import functools

import numpy as np
import jax
import jax.numpy as jnp
from jax import lax
from jax.experimental import pallas as pl
from jax.experimental.pallas import tpu as pltpu

D_MODEL = 2048
BATCH = 16
SEQ = 2048
DEPTH = 2
GRID_W = 64
GRID_H = SEQ // GRID_W
CTX_LEN = 256
MLA_HEADS = 8
QK_NOPE = 128
QK_ROPE = 64
V_HEAD = 128
Q_LORA = 512
KV_LORA = 512
N_FREQ = QK_ROPE // 4
ROPE_THETA = 10000.0
MLA_SCALE = (QK_NOPE + QK_ROPE) ** -0.5
NA_HEADS = 8
NA_HEAD_DIM = 128
WIN_H = 8
WIN_W = 16
NA_SCALE = NA_HEAD_DIM ** -0.5
LOG2E = 1.4426950408889634
MLA_QSCALE = MLA_SCALE * LOG2E
NA_QSCALE = NA_SCALE * LOG2E
D_MLA_OUT = MLA_HEADS * V_HEAD
D_NA = NA_HEADS * NA_HEAD_DIM
D_FF = 5632
N_MOD = 9
EPS = 1e-6
NEG = -1e30

F32 = jnp.float32
BF16 = jnp.bfloat16

LANES = 128
V7X_VMEM_LIMIT = 60000 * 1024

MOD_ROWS = 24
CTX_MOD_ROW = BATCH
QK_PAD = 256
P1_COLS = 2 * Q_LORA + 2 * LANES
NA_GROUP_ROWS = 4
NA_WIN_ROWS = 12
N_NA_GROUPS = GRID_H // NA_GROUP_ROWS

CAST_BLOCK_ELEMS = 1024 * 1024
ROW_CHUNK = 16
MIX_TM = 256
TM = 512
FFN_TM = 1024
FFN_MC = 512
FFN_NC = 512
TF = 512
TQ = 2048
MLA_QC = 512


def _params(sem, vmem=V7X_VMEM_LIMIT):
    return pltpu.CompilerParams(dimension_semantics=sem, vmem_limit_bytes=vmem)


def _rms(x, g):
    return x * lax.rsqrt(jnp.mean(x * x, axis=-1, keepdims=True) + EPS) * g


def _silu(a):
    return a * (1.0 / (1.0 + jnp.exp(-a)))


def _dot(a, b):
    return jnp.dot(a, b, preferred_element_type=F32)


def _dot_nt(a, b):
    return lax.dot_general(a, b, (((1,), (1,)), ((), ())), preferred_element_type=F32)


def _mod_kernel(s_ref, w_ref, b_ref, o_ref):
    s = _silu(s_ref[...])
    o_ref[...] = _dot(s.astype(BF16), w_ref[...].astype(BF16)) + b_ref[...]


def _modulation(c_rows, ada_w, ada_b):
    tn = 1024
    n = N_MOD * D_MODEL
    out = pl.pallas_call(
        _mod_kernel,
        grid=(DEPTH, n // tn),
        in_specs=[
            pl.BlockSpec((MOD_ROWS, D_MODEL), lambda l, j: (0, 0)),
            pl.BlockSpec((None, D_MODEL, tn), lambda l, j: (l, 0, j)),
            pl.BlockSpec((None, 1, tn), lambda l, j: (l, 0, j)),
        ],
        out_specs=pl.BlockSpec((None, MOD_ROWS, tn), lambda l, j: (l, 0, j)),
        out_shape=jax.ShapeDtypeStruct((DEPTH, MOD_ROWS, n), F32),
        compiler_params=_params(("parallel", "parallel")),
        name="adaln_mod",
    )(c_rows, ada_w, ada_b.reshape(DEPTH, 1, n))
    return out.reshape(DEPTH * MOD_ROWS, N_MOD, D_MODEL)


def _norm_modulate_rows(x_ref, g_ref, mod_ref, k0, xn_ref):
    gain = g_ref[...] * (1.0 + mod_ref[0, k0 + 1:k0 + 2, :])
    shift = mod_ref[0, k0:k0 + 1, :]

    def body(r, carry):
        rows = pl.ds(pl.multiple_of(r * ROW_CHUNK, ROW_CHUNK), ROW_CHUNK)
        x = x_ref[rows, :]
        y = x * lax.rsqrt(jnp.mean(x * x, axis=-1, keepdims=True) + EPS)
        xn_ref[rows, :] = (y * gain + shift).astype(BF16)
        return carry

    lax.fori_loop(0, x_ref.shape[0] // ROW_CHUNK, body, 0, unroll=8)


def _mod_spec(row_of_tile, tm):
    return pl.BlockSpec((1, N_MOD, D_MODEL), lambda i, *_: (row_of_tile(i, tm), 0, 0))


def _ffn_kernel(x_ref, mod_ref, g_ref, w1_ref, w3_ref, w2_ref, gf_ref, o_ref, xn_ref, *, k0, final):
    j = pl.program_id(1)

    def step(first):
        if first:
            gain = g_ref[...] * (1.0 + mod_ref[0, k0 + 1:k0 + 2, :])
            shift = mod_ref[0, k0:k0 + 1, :]
        for m in range(FFN_TM // FFN_MC):
            if first:
                for r in range(m * FFN_MC, (m + 1) * FFN_MC, ROW_CHUNK):
                    x = x_ref[r:r + ROW_CHUNK, :]
                    y = x * lax.rsqrt(jnp.mean(x * x, axis=-1, keepdims=True) + EPS)
                    xn_ref[r:r + ROW_CHUNK, :] = (y * gain + shift).astype(BF16)
            rows = slice(m * FFN_MC, (m + 1) * FFN_MC)
            xn = xn_ref[rows, :]
            a = _dot(xn, w1_ref[...])
            b = _dot(xn, w3_ref[...])
            gated = (_silu(a) * b).astype(BF16)
            for n in range(D_MODEL // FFN_NC):
                cols = slice(n * FFN_NC, (n + 1) * FFN_NC)
                part = _dot(gated, w2_ref[:, cols])
                if first:
                    o_ref[rows, cols] = part
                else:
                    o_ref[rows, cols] += part

    pl.when(j == 0)(functools.partial(step, True))
    pl.when(j > 0)(functools.partial(step, False))

    @pl.when(j == pl.num_programs(1) - 1)
    def _():
        half_gate = 0.5 * mod_ref[0, k0 + 2:k0 + 3, :]
        if final:
            for r in range(0, FFN_TM, ROW_CHUNK):
                rows = slice(r, r + ROW_CHUNK)
                o_ref[rows, :] = _rms(x_ref[rows, :] + half_gate * o_ref[rows, :], gf_ref[...])
        else:
            o_ref[...] = x_ref[...] + half_gate * o_ref[...]


def _cast_kernel(w_ref, o_ref):
    o_ref[...] = w_ref[...].astype(BF16)


def _cast_bf16(w):
    depth, rows, cols = w.shape
    tr = 1 << ((CAST_BLOCK_ELEMS // cols).bit_length() - 1)
    assert rows % tr == 0
    spec = pl.BlockSpec((None, tr, cols), lambda l, i: (l, i, 0))
    return pl.pallas_call(
        _cast_kernel,
        grid=(depth, rows // tr),
        in_specs=[spec],
        out_specs=spec,
        out_shape=jax.ShapeDtypeStruct(w.shape, BF16),
        compiler_params=_params(("parallel", "parallel")),
        name="cast_bf16",
    )(w)


def _ffn(x, mod, row_of_tile, g, w1, w3, w2, l, k0, final_g=None):
    t = x.shape[0]
    final = final_g is not None
    gf = final_g if final else g
    return pl.pallas_call(
        functools.partial(_ffn_kernel, k0=k0, final=final),
        grid=(t // FFN_TM, D_FF // TF),
        in_specs=[
            pl.BlockSpec((FFN_TM, D_MODEL), lambda i, j: (i, 0)),
            _mod_spec(row_of_tile, FFN_TM),
            pl.BlockSpec((1, D_MODEL), lambda i, j: (0, 0)),
            pl.BlockSpec((None, D_MODEL, TF), lambda i, j: (l, 0, j)),
            pl.BlockSpec((None, D_MODEL, TF), lambda i, j: (l, 0, j)),
            pl.BlockSpec((None, TF, D_MODEL), lambda i, j: (l, j, 0)),
            pl.BlockSpec((1, D_MODEL), lambda i, j: (0, 0)),
        ],
        out_specs=pl.BlockSpec((FFN_TM, D_MODEL), lambda i, j: (i, 0)),
        out_shape=jax.ShapeDtypeStruct((t, D_MODEL), F32),
        scratch_shapes=[pltpu.VMEM((FFN_TM, D_MODEL), BF16)],
        compiler_params=_params(("parallel", "arbitrary")),
        name="ffn",
    )(x, mod, g.reshape(1, D_MODEL), w1, w3, w2, gf.reshape(1, D_MODEL))


def _mixer_in_kernel(x_ref, mod_ref, g_ref, wp1_ref, wp2_ref, gq_ref, gkv_ref, wa_ref, wb_ref, wk_ref, wv_ref,
                     cos_ref, sin_ref, q_ref, k_ref, v_ref, p2_ref, xn_ref, *, k0):
    _norm_modulate_rows(x_ref, g_ref, mod_ref, k0, xn_ref)
    xn = xn_ref[...]

    for n in range(3):
        cols = slice(n * D_NA, (n + 1) * D_NA)
        y2 = _dot(xn, wp2_ref[:, cols])
        p2_ref[:, cols] = (y2 * NA_QSCALE if n == 0 else y2).astype(BF16)

    p1 = _dot(xn, wp1_ref[...])
    cos, sin = cos_ref[...], sin_ref[...]

    cqn = _rms(p1[:, :Q_LORA], gq_ref[...]).astype(BF16)
    ya = _dot(cqn, wa_ref[...])
    yb = _dot(cqn, wb_ref[...])
    for h in range(MLA_HEADS):
        lo = h * QK_PAD
        q_ref[:, lo:lo + LANES] = (ya[:, lo:lo + LANES] * MLA_QSCALE).astype(BF16)
        rope = ya[:, lo + LANES:lo + QK_PAD] * cos + yb[:, h * LANES:(h + 1) * LANES] * sin
        q_ref[:, lo + LANES:lo + QK_PAD] = (rope * MLA_QSCALE).astype(BF16)

    ckvn = _rms(p1[:, Q_LORA:Q_LORA + KV_LORA], gkv_ref[...]).astype(BF16)
    yk = _dot(ckvn, wk_ref[...])
    v_ref[...] = _dot(ckvn, wv_ref[...]).astype(BF16)
    kr_lo = Q_LORA + KV_LORA
    kr = (p1[:, kr_lo:kr_lo + LANES] * cos + p1[:, kr_lo + LANES:kr_lo + 2 * LANES] * sin).astype(BF16)
    for h in range(MLA_HEADS):
        lo = h * QK_PAD
        k_ref[:, lo:lo + LANES] = yk[:, h * LANES:(h + 1) * LANES].astype(BF16)
        k_ref[:, lo + LANES:lo + QK_PAD] = kr


def _mixer_in(x, mod, row_of_tile, table_block, g, w_p1, w_p2, gq, gkv, wa, wb, wk, wv, cos_t, sin_t, k0):
    t = x.shape[0]
    tm = MIX_TM

    def resident(shape):
        return pl.BlockSpec(shape, lambda i: (0,) * len(shape), pipeline_mode=pl.Buffered(1))

    return pl.pallas_call(
        functools.partial(_mixer_in_kernel, k0=k0),
        grid=(t // tm,),
        in_specs=[
            pl.BlockSpec((tm, D_MODEL), lambda i: (i, 0)),
            _mod_spec(row_of_tile, tm),
            resident((1, D_MODEL)),
            resident((D_MODEL, P1_COLS)),
            resident((D_MODEL, 3 * D_NA)),
            resident((1, Q_LORA)),
            resident((1, KV_LORA)),
            resident((Q_LORA, MLA_HEADS * QK_PAD)),
            resident((Q_LORA, MLA_HEADS * LANES)),
            resident((KV_LORA, MLA_HEADS * QK_NOPE)),
            resident((KV_LORA, D_MLA_OUT)),
            pl.BlockSpec((tm, LANES), lambda i: (table_block(i, tm), 0)),
            pl.BlockSpec((tm, LANES), lambda i: (table_block(i, tm), 0)),
        ],
        out_specs=[
            pl.BlockSpec((tm, MLA_HEADS * QK_PAD), lambda i: (i, 0)),
            pl.BlockSpec((tm, MLA_HEADS * QK_PAD), lambda i: (i, 0)),
            pl.BlockSpec((tm, D_MLA_OUT), lambda i: (i, 0)),
            pl.BlockSpec((tm, 3 * D_NA), lambda i: (i, 0)),
        ],
        out_shape=[
            jax.ShapeDtypeStruct((t, MLA_HEADS * QK_PAD), BF16),
            jax.ShapeDtypeStruct((t, MLA_HEADS * QK_PAD), BF16),
            jax.ShapeDtypeStruct((t, D_MLA_OUT), BF16),
            jax.ShapeDtypeStruct((t, 3 * D_NA), BF16),
        ],
        scratch_shapes=[pltpu.VMEM((tm, D_MODEL), BF16)],
        compiler_params=_params(("parallel",)),
        name="mixer_in",
    )(x, mod, g.reshape(1, D_MODEL), w_p1, w_p2, gq.reshape(1, Q_LORA), gkv.reshape(1, KV_LORA), wa, wb, wk, wv,
      cos_t, sin_t)


def _softmax_pv(s_parts, v_parts):
    m = functools.reduce(jnp.maximum, [jnp.max(s, axis=-1, keepdims=True) for s in s_parts])
    p_parts = [jnp.exp2(s - m) for s in s_parts]
    l = functools.reduce(jnp.add, [jnp.sum(p, axis=-1, keepdims=True) for p in p_parts])
    o = functools.reduce(jnp.add, [_dot(p.astype(BF16), v) for p, v in zip(p_parts, v_parts)])
    return o * (1.0 / l)


def _softmax_pv_ones(s_parts, va_parts):
    m = functools.reduce(jnp.maximum, [jnp.max(s, axis=-1, keepdims=True) for s in s_parts])
    o = functools.reduce(jnp.add, [_dot(jnp.exp2(s - m).astype(BF16), va) for s, va in zip(s_parts, va_parts)])
    return o[:, :V_HEAD] * (1.0 / o[:, V_HEAD:])


def _widen_values(v_ref, va_ref):
    va_ref[:, :V_HEAD] = v_ref[...]
    va_ref[:, V_HEAD:] = jnp.ones(v_ref.shape, BF16)


def _mla_kernel(q_ref, kl_ref, kc_ref, vl_ref, vc_ref, o_ref, vla_ref, vca_ref):
    @pl.when(pl.program_id(2) == 0)
    def _():
        _widen_values(vl_ref, vla_ref)
        _widen_values(vc_ref, vca_ref)

    def scores(c):
        q = q_ref[c * MLA_QC:(c + 1) * MLA_QC, :]
        return [_dot_nt(q, kl_ref[...]), _dot_nt(q, kc_ref[...])]

    n_chunks = TQ // MLA_QC
    s = scores(0)
    for c in range(n_chunks):
        s_next = scores(c + 1) if c + 1 < n_chunks else None
        o = _softmax_pv_ones(s, [vla_ref[...], vca_ref[...]])
        o_ref[c * MLA_QC:(c + 1) * MLA_QC, :] = o.astype(o_ref.dtype)
        s = s_next


def _mla_latent(q, k_lat, k_ctx, v_lat, v_ctx):
    nq = SEQ // TQ
    return pl.pallas_call(
        _mla_kernel,
        grid=(BATCH, MLA_HEADS, nq),
        in_specs=[
            pl.BlockSpec((TQ, QK_PAD), lambda b, h, i: (b * nq + i, h)),
            pl.BlockSpec((SEQ, QK_PAD), lambda b, h, i: (b, h)),
            pl.BlockSpec((CTX_LEN, QK_PAD), lambda b, h, i: (b, h)),
            pl.BlockSpec((SEQ, V_HEAD), lambda b, h, i: (b, h)),
            pl.BlockSpec((CTX_LEN, V_HEAD), lambda b, h, i: (b, h)),
        ],
        out_specs=pl.BlockSpec((TQ, V_HEAD), lambda b, h, i: (b * nq + i, h)),
        out_shape=jax.ShapeDtypeStruct((BATCH * SEQ, D_MLA_OUT), BF16),
        scratch_shapes=[pltpu.VMEM((SEQ, 2 * V_HEAD), BF16), pltpu.VMEM((CTX_LEN, 2 * V_HEAD), BF16)],
        compiler_params=_params(("parallel", "parallel", "arbitrary")),
        name="mla_latent",
    )(q, k_lat, k_ctx, v_lat, v_ctx)


def _ctx_attn_kernel(q_ref, k_ref, v_ref, o_ref, *, width):
    for h in range(MLA_HEADS):
        qk = slice(h * width, (h + 1) * width)
        vo = slice(h * V_HEAD, (h + 1) * V_HEAD)
        o_ref[:, vo] = _softmax_pv([_dot_nt(q_ref[:, qk], k_ref[:, qk])], [v_ref[:, vo]]).astype(o_ref.dtype)


def _ctx_attention(q, k, v, width, q_off, k_off, v_off):
    return pl.pallas_call(
        functools.partial(_ctx_attn_kernel, width=width),
        grid=(BATCH,),
        in_specs=[
            pl.BlockSpec((CTX_LEN, MLA_HEADS * width), lambda b: (b, q_off)),
            pl.BlockSpec((CTX_LEN, MLA_HEADS * width), lambda b: (b, k_off)),
            pl.BlockSpec((CTX_LEN, MLA_HEADS * V_HEAD), lambda b: (b, v_off)),
        ],
        out_specs=pl.BlockSpec((CTX_LEN, MLA_HEADS * V_HEAD), lambda b: (b, 0)),
        out_shape=jax.ShapeDtypeStruct((BATCH * CTX_LEN, MLA_HEADS * V_HEAD), BF16),
        compiler_params=_params(("parallel",)),
        name="ctx_attention",
    )(q, k, v)


def _na_group_window(g):
    if g == 0:
        return 0, 0
    if g == 1:
        return 1, 0
    if g == N_NA_GROUPS - 1:
        return 3, GRID_H - NA_WIN_ROWS
    return 2, NA_GROUP_ROWS * g - WIN_H // 2


def _na_kernel(q_ref, k_ref, v_ref, kc_ref, vc_ref, tab_ref, o_ref, va_ref, vca_ref):
    gq = NA_GROUP_ROWS * GRID_W
    gk = NA_WIN_ROWS * GRID_W
    _widen_values(v_ref, va_ref)
    _widen_values(vc_ref, vca_ref)

    def scores(g):
        var, ks = _na_group_window(g)
        q = q_ref[g * gq:(g + 1) * gq, :]
        kw = k_ref[ks * GRID_W:ks * GRID_W + gk, :]
        return [_dot_nt(q, kw) + tab_ref[var], _dot_nt(q, kc_ref[...])]

    s = scores(0)
    for g in range(N_NA_GROUPS):
        s_next = scores(g + 1) if g + 1 < N_NA_GROUPS else None
        _, ks = _na_group_window(g)
        vw = va_ref[ks * GRID_W:ks * GRID_W + gk, :]
        o_ref[g * gq:(g + 1) * gq, :] = _softmax_pv_ones(s, [vw, vca_ref[...]]).astype(o_ref.dtype)
        s = s_next


def _na_tables(rpb):
    nh, nr, nc = rpb.shape
    v = jnp.pad(rpb.astype(F32), ((0, 0), (0, 0), (GRID_W - WIN_W, GRID_W + WIN_W - nc)))
    flat = jnp.broadcast_to(v[:, :, None, :], (nh, nr, GRID_W, 2 * GRID_W)).reshape(nh, nr, 2 * GRID_W * GRID_W)
    toe = flat[:, :, :GRID_W * (2 * GRID_W - 1)].reshape(nh, nr, GRID_W, 2 * GRID_W - 1)[..., GRID_W - 1:]
    row_pad = WIN_H // 2
    toe = jnp.pad(toe, ((0, 0), (row_pad, row_pad), (0, 0), (0, 0)))

    qa = np.arange(NA_GROUP_ROWS).reshape(-1, 1, 1, 1)
    qc = np.arange(GRID_W).reshape(1, -1, 1, 1)
    ki = np.arange(NA_WIN_ROWS).reshape(1, 1, -1, 1)
    kc = np.arange(GRID_W).reshape(1, 1, 1, -1)
    col_start = np.clip(qc - WIN_W // 2, 0, GRID_W - WIN_W)
    col_ok = (kc >= col_start) & (kc < col_start + WIN_W)
    tables = []
    for g in (0, 1, 2, N_NA_GROUPS - 1):
        _, ks = _na_group_window(g)
        r = NA_GROUP_ROWS * g + qa
        rs = np.clip(r - WIN_H // 2, 0, GRID_H - WIN_H)
        krow = ks + ki
        ok = np.broadcast_to((krow >= rs) & (krow < rs + WIN_H) & col_ok,
                             (NA_GROUP_ROWS, GRID_W, NA_WIN_ROWS, GRID_W))
        per_row = []
        for a in range(NA_GROUP_ROWS):
            base = ks - (NA_GROUP_ROWS * g + a) + WIN_H - 1 + row_pad
            assert 0 <= base and base + NA_WIN_ROWS <= nr + 2 * row_pad
            per_row.append(toe[:, base:base + NA_WIN_ROWS])
        bias = jnp.stack(per_row, axis=1).transpose(0, 1, 3, 2, 4)
        bias = jnp.where(ok[None], bias * LOG2E, NEG)
        tables.append(bias.reshape(nh, NA_GROUP_ROWS * GRID_W, NA_WIN_ROWS * GRID_W))
    return jnp.stack(tables)


def _na_latent(p2_lat, p2_ctx, tables, l):
    nh = NA_HEADS
    gq = NA_GROUP_ROWS * GRID_W
    gk = NA_WIN_ROWS * GRID_W
    return pl.pallas_call(
        _na_kernel,
        grid=(nh, BATCH),
        in_specs=[
            pl.BlockSpec((SEQ, NA_HEAD_DIM), lambda h, b: (b, h)),
            pl.BlockSpec((SEQ, NA_HEAD_DIM), lambda h, b: (b, nh + h)),
            pl.BlockSpec((SEQ, NA_HEAD_DIM), lambda h, b: (b, 2 * nh + h)),
            pl.BlockSpec((CTX_LEN, NA_HEAD_DIM), lambda h, b: (b, nh + h)),
            pl.BlockSpec((CTX_LEN, NA_HEAD_DIM), lambda h, b: (b, 2 * nh + h)),
            pl.BlockSpec((4, None, gq, gk), lambda h, b: (0, l * nh + h, 0, 0)),
        ],
        out_specs=pl.BlockSpec((SEQ, NA_HEAD_DIM), lambda h, b: (b, h)),
        out_shape=jax.ShapeDtypeStruct((BATCH * SEQ, D_NA), BF16),
        scratch_shapes=[pltpu.VMEM((SEQ, 2 * V_HEAD), BF16), pltpu.VMEM((CTX_LEN, 2 * V_HEAD), BF16)],
        compiler_params=_params(("parallel", "parallel")),
        name="na_latent",
    )(p2_lat, p2_lat, p2_lat, p2_ctx, p2_ctx, tables)


def _merge_kernel(a_ref, b_ref, h_ref, mod_ref, ga_ref, gb_ref, wa_ref, wb_ref, o_ref, *, k0):
    an = _rms(a_ref[...].astype(F32), ga_ref[...]).astype(BF16)
    bn = _rms(b_ref[...].astype(F32), gb_ref[...]).astype(BF16)
    y = _dot(an, wa_ref[...]) + _dot(bn, wb_ref[...])
    o_ref[...] = h_ref[...] + mod_ref[0, k0:k0 + 1, :] * y


def _merge(a, b, h, mod, row_of_tile, out_norm, w_out, k0):
    t = h.shape[0]
    g2 = out_norm.reshape(2, 1, D_MLA_OUT)
    return pl.pallas_call(
        functools.partial(_merge_kernel, k0=k0),
        grid=(t // TM,),
        in_specs=[
            pl.BlockSpec((TM, D_MLA_OUT), lambda i: (i, 0)),
            pl.BlockSpec((TM, D_NA), lambda i: (i, 0)),
            pl.BlockSpec((TM, D_MODEL), lambda i: (i, 0)),
            _mod_spec(row_of_tile, TM),
            pl.BlockSpec((None, 1, D_MLA_OUT), lambda i: (0, 0, 0)),
            pl.BlockSpec((None, 1, D_NA), lambda i: (1, 0, 0)),
            pl.BlockSpec((D_MLA_OUT, D_MODEL), lambda i: (0, 0)),
            pl.BlockSpec((D_NA, D_MODEL), lambda i: (1, 0)),
        ],
        out_specs=pl.BlockSpec((TM, D_MODEL), lambda i: (i, 0)),
        out_shape=jax.ShapeDtypeStruct((t, D_MODEL), F32),
        compiler_params=_params(("parallel",)),
        name="merge",
    )(a, b, h, mod, g2, g2, w_out, w_out)


def _rope_tables():
    t = np.arange(SEQ)
    pos = np.stack([t // GRID_W, t % GRID_W], axis=-1).astype(np.float32)
    inv_freq = jnp.asarray(ROPE_THETA, F32) ** (-jnp.arange(N_FREQ, dtype=F32) / N_FREQ)
    ang = jnp.asarray(pos)[:, :, None] * inv_freq
    cos, sin = jnp.cos(ang), jnp.sin(ang)
    cos_t = jnp.concatenate([cos, cos], axis=-1).reshape(SEQ, QK_ROPE)
    sin_t = jnp.concatenate([-sin, sin], axis=-1).reshape(SEQ, QK_ROPE)
    pad = jnp.zeros((SEQ, LANES - QK_ROPE), F32)
    cos_t = jnp.concatenate([cos_t, pad], axis=-1)
    sin_t = jnp.concatenate([sin_t, pad], axis=-1)
    ident = jnp.concatenate([jnp.ones((TM, QK_ROPE), F32), jnp.zeros((TM, LANES - QK_ROPE), F32)], axis=-1)
    return (jnp.concatenate([cos_t, ident], axis=0),
            jnp.concatenate([sin_t, jnp.zeros((TM, LANES), F32)], axis=0))


def _rotate_half_cols(w):
    blocks = w.reshape(w.shape[:-1] + (2, 2, N_FREQ))
    return blocks[..., ::-1, :].reshape(w.shape)


def _layer_weights(w_in, w_uq, w_ukv):
    kr = w_in[:, 2 * Q_LORA:2 * Q_LORA + QK_ROPE]
    zpad = jnp.zeros((D_MODEL, LANES - QK_ROPE), F32)
    w_p1 = jnp.concatenate([w_in[:, :2 * Q_LORA], kr, zpad, _rotate_half_cols(kr), zpad], axis=1).astype(BF16)
    w_p2 = w_in[:, 2 * Q_LORA + QK_ROPE:].astype(BF16)
    uq = w_uq.reshape(Q_LORA, MLA_HEADS, QK_NOPE + QK_ROPE)
    hz = jnp.zeros((Q_LORA, MLA_HEADS, QK_PAD - QK_NOPE - QK_ROPE), F32)
    wa = jnp.concatenate([uq, hz], axis=-1).reshape(Q_LORA, MLA_HEADS * QK_PAD).astype(BF16)
    wb = jnp.concatenate([_rotate_half_cols(uq[..., QK_NOPE:]), hz], axis=-1)
    wb = wb.reshape(Q_LORA, MLA_HEADS * LANES).astype(BF16)
    ukv = w_ukv.reshape(KV_LORA, MLA_HEADS, QK_NOPE + V_HEAD)
    wk = ukv[..., :QK_NOPE].reshape(KV_LORA, MLA_HEADS * QK_NOPE).astype(BF16)
    wv = ukv[..., QK_NOPE:].reshape(KV_LORA, D_MLA_OUT).astype(BF16)
    return w_p1, w_p2, wa, wb, wk, wv


def kernel(x, c, ctx, c_ctx, ada_w, ada_b, ffn1_norm, ffn1_w1, ffn1_w3, ffn1_w2, mix_norm, w_in, qa_norm, w_uq,
           kva_norm, w_ukv, na_rpb, out_norm, w_out, ffn2_norm, ffn2_w1, ffn2_w3, ffn2_w2, final_norm):
    c_rows = jnp.concatenate([c, c_ctx[None, :], jnp.zeros((MOD_ROWS - BATCH - 1, D_MODEL), F32)], axis=0)
    mod = _modulation(c_rows, ada_w, ada_b)
    cos_t, sin_t = _rope_tables()
    lat_tab = lambda i, tm: i % (SEQ // tm)
    ctx_tab = lambda i, tm: SEQ // tm
    tables = _na_tables(na_rpb.reshape(DEPTH * NA_HEADS, 2 * WIN_H - 1, 2 * WIN_W - 1))
    ffn1_w = tuple(_cast_bf16(w) for w in (ffn1_w1, ffn1_w3, ffn1_w2))
    ffn2_w = tuple(_cast_bf16(w) for w in (ffn2_w1, ffn2_w3, ffn2_w2))

    h = x.reshape(BATCH * SEQ, D_MODEL)
    hc = ctx.reshape(BATCH * CTX_LEN, D_MODEL)
    for l in range(DEPTH):
        last = l == DEPTH - 1
        lat_row = lambda i, tm, l=l: l * MOD_ROWS + (i * tm) // SEQ
        ctx_row = lambda i, tm, l=l: l * MOD_ROWS + CTX_MOD_ROW
        h = _ffn(h, mod, lat_row, ffn1_norm[l], *ffn1_w, l, 0)
        hc = _ffn(hc, mod, ctx_row, ffn1_norm[l], *ffn1_w, l, 0)

        w_p1, w_p2, wa, wb, wk, wv = _layer_weights(w_in[l], w_uq[l], w_ukv[l])
        mix_w = (mix_norm[l], w_p1, w_p2, qa_norm[l], kva_norm[l], wa, wb, wk, wv, cos_t, sin_t, 3)
        q, k, v, p2 = _mixer_in(h, mod, lat_row, lat_tab, *mix_w)
        qc, kc, vc, p2c = _mixer_in(hc, mod, ctx_row, ctx_tab, *mix_w)
        a_lat = _mla_latent(q, k, kc, v, vc)
        b_lat = _na_latent(p2, p2c, tables, l)
        h = _merge(a_lat, b_lat, h, mod, lat_row, out_norm[l], w_out[l].astype(BF16), 5)
        h = _ffn(h, mod, lat_row, ffn2_norm[l], *ffn2_w, l, 6, final_g=final_norm if last else None)
        if not last:
            a_ctx = _ctx_attention(qc, kc, vc, QK_PAD, 0, 0, 0)
            b_ctx = _ctx_attention(p2c, p2c, p2c, NA_HEAD_DIM, 0, 1, 2)
            hc = _merge(a_ctx, b_ctx, hc, mod, ctx_row, out_norm[l], w_out[l].astype(BF16), 5)
            hc = _ffn(hc, mod, ctx_row, ffn2_norm[l], *ffn2_w, l, 6)
    return h.reshape(BATCH, SEQ, D_MODEL)
```

```python
import functools

import numpy as np
import jax
import jax.numpy as jnp
from jax import lax
from jax.experimental import pallas as pl
from jax.experimental.pallas import tpu as pltpu

D_MODEL = 2048
BATCH = 16
SEQ = 2048
DEPTH = 2
GRID_W = 64
GRID_H = SEQ // GRID_W
CTX_LEN = 256
MLA_HEADS = 8
QK_NOPE = 128
QK_ROPE = 64
V_HEAD = 128
Q_LORA = 512
KV_LORA = 512
N_FREQ = QK_ROPE // 4
ROPE_THETA = 10000.0
MLA_SCALE = (QK_NOPE + QK_ROPE) ** -0.5
NA_HEADS = 8
NA_HEAD_DIM = 128
WIN_H = 8
WIN_W = 16
NA_SCALE = NA_HEAD_DIM ** -0.5
LOG2E = 1.4426950408889634
MLA_QSCALE = MLA_SCALE * LOG2E
NA_QSCALE = NA_SCALE * LOG2E
D_MLA_OUT = MLA_HEADS * V_HEAD
D_NA = NA_HEADS * NA_HEAD_DIM
D_FF = 5632
N_MOD = 9
EPS = 1e-6
NEG = -1e30

F32 = jnp.float32
BF16 = jnp.bfloat16

LANES = 128
V7X_VMEM_LIMIT = 60000 * 1024

MOD_ROWS = 24
CTX_MOD_ROW = BATCH
QK_PAD = 256
P1_COLS = 2 * Q_LORA + 2 * LANES
NA_GROUP_ROWS = 4
NA_WIN_ROWS = 12
N_NA_GROUPS = GRID_H // NA_GROUP_ROWS

CAST_BLOCK_ELEMS = 1024 * 1024
ROW_CHUNK = 16
MIX_TM = 256
TM = 512
FFN_TM = 1024
FFN_MC = 512
FFN_NC = 512
TF = 512
TQ = 2048
MLA_QC = 512


def _params(sem, vmem=V7X_VMEM_LIMIT):
    return pltpu.CompilerParams(dimension_semantics=sem, vmem_limit_bytes=vmem)


def _rms(x, g):
    return x * lax.rsqrt(jnp.mean(x * x, axis=-1, keepdims=True) + EPS) * g


def _silu(a):
    return a * (1.0 / (1.0 + jnp.exp(-a)))


def _dot(a, b):
    return jnp.dot(a, b, preferred_element_type=F32)


def _dot_nt(a, b):
    return lax.dot_general(a, b, (((1,), (1,)), ((), ())), preferred_element_type=F32)


def _mod_kernel(s_ref, w_ref, b_ref, o_ref):
    s = _silu(s_ref[...])
    o_ref[...] = _dot(s.astype(BF16), w_ref[...].astype(BF16)) + b_ref[...]


def _modulation(c_rows, ada_w, ada_b):
    tn = 1024
    n = N_MOD * D_MODEL
    out = pl.pallas_call(
        _mod_kernel,
        grid=(DEPTH, n // tn),
        in_specs=[
            pl.BlockSpec((MOD_ROWS, D_MODEL), lambda l, j: (0, 0)),
            pl.BlockSpec((None, D_MODEL, tn), lambda l, j: (l, 0, j)),
            pl.BlockSpec((None, 1, tn), lambda l, j: (l, 0, j)),
        ],
        out_specs=pl.BlockSpec((None, MOD_ROWS, tn), lambda l, j: (l, 0, j)),
        out_shape=jax.ShapeDtypeStruct((DEPTH, MOD_ROWS, n), F32),
        compiler_params=_params(("parallel", "parallel")),
        name="adaln_mod",
    )(c_rows, ada_w, ada_b.reshape(DEPTH, 1, n))
    return out.reshape(DEPTH * MOD_ROWS, N_MOD, D_MODEL)


def _norm_modulate_rows(x_ref, g_ref, mod_ref, k0, xn_ref):
    gain = g_ref[...] * (1.0 + mod_ref[0, k0 + 1:k0 + 2, :])
    shift = mod_ref[0, k0:k0 + 1, :]

    def body(r, carry):
        rows = pl.ds(pl.multiple_of(r * ROW_CHUNK, ROW_CHUNK), ROW_CHUNK)
        x = x_ref[rows, :]
        y = x * lax.rsqrt(jnp.mean(x * x, axis=-1, keepdims=True) + EPS)
        xn_ref[rows, :] = (y * gain + shift).astype(BF16)
        return carry

    lax.fori_loop(0, x_ref.shape[0] // ROW_CHUNK, body, 0, unroll=8)


def _mod_spec(row_of_tile, tm):
    return pl.BlockSpec((1, N_MOD, D_MODEL), lambda i, *_: (row_of_tile(i, tm), 0, 0))


def _ffn_kernel(x_ref, mod_ref, g_ref, w1_ref, w3_ref, w2_ref, gf_ref, o_ref, xn_ref, *, k0, final):
    j = pl.program_id(1)

    def step(first, last):
        if last:
            half_gate = 0.5 * mod_ref[0, k0 + 2:k0 + 3, :]
        if first:
            gain = g_ref[...] * (1.0 + mod_ref[0, k0 + 1:k0 + 2, :])
            shift = mod_ref[0, k0:k0 + 1, :]
        for m in range(FFN_TM // FFN_MC):
            if first:
                for r in range(m * FFN_MC, (m + 1) * FFN_MC, ROW_CHUNK):
                    x = x_ref[r:r + ROW_CHUNK, :]
                    y = x * lax.rsqrt(jnp.mean(x * x, axis=-1, keepdims=True) + EPS)
                    xn_ref[r:r + ROW_CHUNK, :] = (y * gain + shift).astype(BF16)
            rows = slice(m * FFN_MC, (m + 1) * FFN_MC)
            xn = xn_ref[rows, :]
            a = _dot(xn, w1_ref[...])
            b = _dot(xn, w3_ref[...])
            gated = (_silu(a) * b).astype(BF16)
            for n in range(D_MODEL // FFN_NC):
                cols = slice(n * FFN_NC, (n + 1) * FFN_NC)
                part = _dot(gated, w2_ref[:, cols])
                if first:
                    o_ref[rows, cols] = part
                elif last:
                    o_ref[rows, cols] = x_ref[rows, cols] + half_gate[:, cols] * (o_ref[rows, cols] + part)
                else:
                    o_ref[rows, cols] += part
            if last and final:
                for r in range(m * FFN_MC, (m + 1) * FFN_MC, ROW_CHUNK):
                    o_ref[r:r + ROW_CHUNK, :] = _rms(o_ref[r:r + ROW_CHUNK, :], gf_ref[...])

    n_steps = pl.num_programs(1)
    pl.when(j == 0)(functools.partial(step, True, False))
    pl.when((j > 0) & (j < n_steps - 1))(functools.partial(step, False, False))
    pl.when(j == n_steps - 1)(functools.partial(step, False, True))


def _cast_kernel(w_ref, o_ref):
    o_ref[...] = w_ref[...].astype(BF16)


def _cast_bf16(w):
    depth, rows, cols = w.shape
    tr = 1 << ((CAST_BLOCK_ELEMS // cols).bit_length() - 1)
    assert rows % tr == 0
    spec = pl.BlockSpec((None, tr, cols), lambda l, i: (l, i, 0))
    return pl.pallas_call(
        _cast_kernel,
        grid=(depth, rows // tr),
        in_specs=[spec],
        out_specs=spec,
        out_shape=jax.ShapeDtypeStruct(w.shape, BF16),
        compiler_params=_params(("parallel", "parallel")),
        name="cast_bf16",
    )(w)


def _ffn(x, mod, row_of_tile, g, w1, w3, w2, l, k0, final_g=None):
    t = x.shape[0]
    final = final_g is not None
    gf = final_g if final else g
    return pl.pallas_call(
        functools.partial(_ffn_kernel, k0=k0, final=final),
        grid=(t // FFN_TM, D_FF // TF),
        in_specs=[
            pl.BlockSpec((FFN_TM, D_MODEL), lambda i, j: (i, 0)),
            _mod_spec(row_of_tile, FFN_TM),
            pl.BlockSpec((1, D_MODEL), lambda i, j: (0, 0)),
            pl.BlockSpec((None, D_MODEL, TF), lambda i, j: (l, 0, j)),
            pl.BlockSpec((None, D_MODEL, TF), lambda i, j: (l, 0, j)),
            pl.BlockSpec((None, TF, D_MODEL), lambda i, j: (l, j, 0)),
            pl.BlockSpec((1, D_MODEL), lambda i, j: (0, 0)),
        ],
        out_specs=pl.BlockSpec((FFN_TM, D_MODEL), lambda i, j: (i, 0)),
        out_shape=jax.ShapeDtypeStruct((t, D_MODEL), F32),
        scratch_shapes=[pltpu.VMEM((FFN_TM, D_MODEL), BF16)],
        compiler_params=_params(("parallel", "arbitrary")),
        name="ffn",
    )(x, mod, g.reshape(1, D_MODEL), w1, w3, w2, gf.reshape(1, D_MODEL))


def _mixer_in_kernel(x_ref, mod_ref, g_ref, wp1_ref, wp2_ref, gq_ref, gkv_ref, wa_ref, wb_ref, wk_ref, wv_ref,
                     cos_ref, sin_ref, q_ref, k_ref, v_ref, p2_ref, xn_ref, *, k0):
    _norm_modulate_rows(x_ref, g_ref, mod_ref, k0, xn_ref)
    xn = xn_ref[...]

    for n in range(3):
        cols = slice(n * D_NA, (n + 1) * D_NA)
        y2 = _dot(xn, wp2_ref[:, cols])
        p2_ref[:, cols] = (y2 * NA_QSCALE if n == 0 else y2).astype(BF16)

    p1 = _dot(xn, wp1_ref[...])
    cos, sin = cos_ref[...], sin_ref[...]

    cqn = _rms(p1[:, :Q_LORA], gq_ref[...]).astype(BF16)
    ya = _dot(cqn, wa_ref[...])
    yb = _dot(cqn, wb_ref[...])
    for h in range(MLA_HEADS):
        lo = h * QK_PAD
        q_ref[:, lo:lo + LANES] = (ya[:, lo:lo + LANES] * MLA_QSCALE).astype(BF16)
        rope = ya[:, lo + LANES:lo + QK_PAD] * cos + yb[:, h * LANES:(h + 1) * LANES] * sin
        q_ref[:, lo + LANES:lo + QK_PAD] = (rope * MLA_QSCALE).astype(BF16)

    ckvn = _rms(p1[:, Q_LORA:Q_LORA + KV_LORA], gkv_ref[...]).astype(BF16)
    yk = _dot(ckvn, wk_ref[...])
    v_ref[...] = _dot(ckvn, wv_ref[...]).astype(BF16)
    kr_lo = Q_LORA + KV_LORA
    kr = (p1[:, kr_lo:kr_lo + LANES] * cos + p1[:, kr_lo + LANES:kr_lo + 2 * LANES] * sin).astype(BF16)
    for h in range(MLA_HEADS):
        lo = h * QK_PAD
        k_ref[:, lo:lo + LANES] = yk[:, h * LANES:(h + 1) * LANES].astype(BF16)
        k_ref[:, lo + LANES:lo + QK_PAD] = kr


def _mixer_in(x, mod, row_of_tile, table_block, g, w_p1, w_p2, gq, gkv, wa, wb, wk, wv, cos_t, sin_t, k0):
    t = x.shape[0]
    tm = MIX_TM

    def resident(shape):
        return pl.BlockSpec(shape, lambda i: (0,) * len(shape), pipeline_mode=pl.Buffered(1))

    return pl.pallas_call(
        functools.partial(_mixer_in_kernel, k0=k0),
        grid=(t // tm,),
        in_specs=[
            pl.BlockSpec((tm, D_MODEL), lambda i: (i, 0)),
            _mod_spec(row_of_tile, tm),
            resident((1, D_MODEL)),
            resident((D_MODEL, P1_COLS)),
            resident((D_MODEL, 3 * D_NA)),
            resident((1, Q_LORA)),
            resident((1, KV_LORA)),
            resident((Q_LORA, MLA_HEADS * QK_PAD)),
            resident((Q_LORA, MLA_HEADS * LANES)),
            resident((KV_LORA, MLA_HEADS * QK_NOPE)),
            resident((KV_LORA, D_MLA_OUT)),
            pl.BlockSpec((tm, LANES), lambda i: (table_block(i, tm), 0)),
            pl.BlockSpec((tm, LANES), lambda i: (table_block(i, tm), 0)),
        ],
        out_specs=[
            pl.BlockSpec((tm, MLA_HEADS * QK_PAD), lambda i: (i, 0)),
            pl.BlockSpec((tm, MLA_HEADS * QK_PAD), lambda i: (i, 0)),
            pl.BlockSpec((tm, D_MLA_OUT), lambda i: (i, 0)),
            pl.BlockSpec((tm, 3 * D_NA), lambda i: (i, 0)),
        ],
        out_shape=[
            jax.ShapeDtypeStruct((t, MLA_HEADS * QK_PAD), BF16),
            jax.ShapeDtypeStruct((t, MLA_HEADS * QK_PAD), BF16),
            jax.ShapeDtypeStruct((t, D_MLA_OUT), BF16),
            jax.ShapeDtypeStruct((t, 3 * D_NA), BF16),
        ],
        scratch_shapes=[pltpu.VMEM((tm, D_MODEL), BF16)],
        compiler_params=_params(("parallel",)),
        name="mixer_in",
    )(x, mod, g.reshape(1, D_MODEL), w_p1, w_p2, gq.reshape(1, Q_LORA), gkv.reshape(1, KV_LORA), wa, wb, wk, wv,
      cos_t, sin_t)


def _softmax_pv(s_parts, v_parts):
    m = functools.reduce(jnp.maximum, [jnp.max(s, axis=-1, keepdims=True) for s in s_parts])
    p_parts = [jnp.exp2(s - m) for s in s_parts]
    l = functools.reduce(jnp.add, [jnp.sum(p, axis=-1, keepdims=True) for p in p_parts])
    o = functools.reduce(jnp.add, [_dot(p.astype(BF16), v) for p, v in zip(p_parts, v_parts)])
    return o * (1.0 / l)


def _softmax_pv_ones(s_parts, va_parts):
    m = functools.reduce(jnp.maximum, [jnp.max(s, axis=-1, keepdims=True) for s in s_parts])
    o = functools.reduce(jnp.add, [_dot(jnp.exp2(s - m).astype(BF16), va) for s, va in zip(s_parts, va_parts)])
    return o[:, :V_HEAD] * (1.0 / o[:, V_HEAD:])


def _widen_values(v_ref, va_ref):
    va_ref[:, :V_HEAD] = v_ref[...]
    va_ref[:, V_HEAD:] = jnp.ones(v_ref.shape, BF16)


def _mla_kernel(q_ref, kl_ref, kc_ref, vl_ref, vc_ref, o_ref, vla_ref, vca_ref):
    @pl.when(pl.program_id(2) == 0)
    def _():
        _widen_values(vl_ref, vla_ref)
        _widen_values(vc_ref, vca_ref)

    def scores(c):
        q = q_ref[c * MLA_QC:(c + 1) * MLA_QC, :]
        return [_dot_nt(q, kl_ref[...]), _dot_nt(q, kc_ref[...])]

    n_chunks = TQ // MLA_QC
    s = scores(0)
    for c in range(n_chunks):
        s_next = scores(c + 1) if c + 1 < n_chunks else None
        o = _softmax_pv_ones(s, [vla_ref[...], vca_ref[...]])
        o_ref[c * MLA_QC:(c + 1) * MLA_QC, :] = o.astype(o_ref.dtype)
        s = s_next


def _mla_latent(q, k_lat, k_ctx, v_lat, v_ctx):
    nq = SEQ // TQ
    return pl.pallas_call(
        _mla_kernel,
        grid=(BATCH, MLA_HEADS, nq),
        in_specs=[
            pl.BlockSpec((TQ, QK_PAD), lambda b, h, i: (b * nq + i, h)),
            pl.BlockSpec((SEQ, QK_PAD), lambda b, h, i: (b, h)),
            pl.BlockSpec((CTX_LEN, QK_PAD), lambda b, h, i: (b, h)),
            pl.BlockSpec((SEQ, V_HEAD), lambda b, h, i: (b, h)),
            pl.BlockSpec((CTX_LEN, V_HEAD), lambda b, h, i: (b, h)),
        ],
        out_specs=pl.BlockSpec((TQ, V_HEAD), lambda b, h, i: (b * nq + i, h)),
        out_shape=jax.ShapeDtypeStruct((BATCH * SEQ, D_MLA_OUT), BF16),
        scratch_shapes=[pltpu.VMEM((SEQ, 2 * V_HEAD), BF16), pltpu.VMEM((CTX_LEN, 2 * V_HEAD), BF16)],
        compiler_params=_params(("parallel", "parallel", "arbitrary")),
        name="mla_latent",
    )(q, k_lat, k_ctx, v_lat, v_ctx)


def _ctx_attn_kernel(q_ref, k_ref, v_ref, o_ref, *, width):
    for h in range(MLA_HEADS):
        qk = slice(h * width, (h + 1) * width)
        vo = slice(h * V_HEAD, (h + 1) * V_HEAD)
        o_ref[:, vo] = _softmax_pv([_dot_nt(q_ref[:, qk], k_ref[:, qk])], [v_ref[:, vo]]).astype(o_ref.dtype)


def _ctx_attention(q, k, v, width, q_off, k_off, v_off):
    return pl.pallas_call(
        functools.partial(_ctx_attn_kernel, width=width),
        grid=(BATCH,),
        in_specs=[
            pl.BlockSpec((CTX_LEN, MLA_HEADS * width), lambda b: (b, q_off)),
            pl.BlockSpec((CTX_LEN, MLA_HEADS * width), lambda b: (b, k_off)),
            pl.BlockSpec((CTX_LEN, MLA_HEADS * V_HEAD), lambda b: (b, v_off)),
        ],
        out_specs=pl.BlockSpec((CTX_LEN, MLA_HEADS * V_HEAD), lambda b: (b, 0)),
        out_shape=jax.ShapeDtypeStruct((BATCH * CTX_LEN, MLA_HEADS * V_HEAD), BF16),
        compiler_params=_params(("parallel",)),
        name="ctx_attention",
    )(q, k, v)


def _na_group_window(g):
    if g == 0:
        return 0, 0
    if g == 1:
        return 1, 0
    if g == N_NA_GROUPS - 1:
        return 3, GRID_H - NA_WIN_ROWS
    return 2, NA_GROUP_ROWS * g - WIN_H // 2


def _na_kernel(q_ref, k_ref, v_ref, kc_ref, vc_ref, tab_ref, o_ref, va_ref, vca_ref):
    gq = NA_GROUP_ROWS * GRID_W
    gk = NA_WIN_ROWS * GRID_W
    _widen_values(v_ref, va_ref)
    _widen_values(vc_ref, vca_ref)

    def scores(g):
        var, ks = _na_group_window(g)
        q = q_ref[g * gq:(g + 1) * gq, :]
        kw = k_ref[ks * GRID_W:ks * GRID_W + gk, :]
        return [_dot_nt(q, kw) + tab_ref[var], _dot_nt(q, kc_ref[...])]

    s = scores(0)
    for g in range(N_NA_GROUPS):
        s_next = scores(g + 1) if g + 1 < N_NA_GROUPS else None
        _, ks = _na_group_window(g)
        vw = va_ref[ks * GRID_W:ks * GRID_W + gk, :]
        o_ref[g * gq:(g + 1) * gq, :] = _softmax_pv_ones(s, [vw, vca_ref[...]]).astype(o_ref.dtype)
        s = s_next


def _na_tables(rpb):
    nh, nr, nc = rpb.shape
    v = jnp.pad(rpb.astype(F32), ((0, 0), (0, 0), (GRID_W - WIN_W, GRID_W + WIN_W - nc)))
    flat = jnp.broadcast_to(v[:, :, None, :], (nh, nr, GRID_W, 2 * GRID_W)).reshape(nh, nr, 2 * GRID_W * GRID_W)
    toe = flat[:, :, :GRID_W * (2 * GRID_W - 1)].reshape(nh, nr, GRID_W, 2 * GRID_W - 1)[..., GRID_W - 1:]
    row_pad = WIN_H // 2
    toe = jnp.pad(toe, ((0, 0), (row_pad, row_pad), (0, 0), (0, 0)))

    qa = np.arange(NA_GROUP_ROWS).reshape(-1, 1, 1, 1)
    qc = np.arange(GRID_W).reshape(1, -1, 1, 1)
    ki = np.arange(NA_WIN_ROWS).reshape(1, 1, -1, 1)
    kc = np.arange(GRID_W).reshape(1, 1, 1, -1)
    col_start = np.clip(qc - WIN_W // 2, 0, GRID_W - WIN_W)
    col_ok = (kc >= col_start) & (kc < col_start + WIN_W)
    tables = []
    for g in (0, 1, 2, N_NA_GROUPS - 1):
        _, ks = _na_group_window(g)
        r = NA_GROUP_ROWS * g + qa
        rs = np.clip(r - WIN_H // 2, 0, GRID_H - WIN_H)
        krow = ks + ki
        ok = np.broadcast_to((krow >= rs) & (krow < rs + WIN_H) & col_ok,
                             (NA_GROUP_ROWS, GRID_W, NA_WIN_ROWS, GRID_W))
        per_row = []
        for a in range(NA_GROUP_ROWS):
            base = ks - (NA_GROUP_ROWS * g + a) + WIN_H - 1 + row_pad
            assert 0 <= base and base + NA_WIN_ROWS <= nr + 2 * row_pad
            per_row.append(toe[:, base:base + NA_WIN_ROWS])
        bias = jnp.stack(per_row, axis=1).transpose(0, 1, 3, 2, 4)
        bias = jnp.where(ok[None], bias * LOG2E, NEG)
        tables.append(bias.reshape(nh, NA_GROUP_ROWS * GRID_W, NA_WIN_ROWS * GRID_W))
    return jnp.stack(tables)


def _na_latent(p2_lat, p2_ctx, tables, l):
    nh = NA_HEADS
    gq = NA_GROUP_ROWS * GRID_W
    gk = NA_WIN_ROWS * GRID_W
    return pl.pallas_call(
        _na_kernel,
        grid=(nh, BATCH),
        in_specs=[
            pl.BlockSpec((SEQ, NA_HEAD_DIM), lambda h, b: (b, h)),
            pl.BlockSpec((SEQ, NA_HEAD_DIM), lambda h, b: (b, nh + h)),
            pl.BlockSpec((SEQ, NA_HEAD_DIM), lambda h, b: (b, 2 * nh + h)),
            pl.BlockSpec((CTX_LEN, NA_HEAD_DIM), lambda h, b: (b, nh + h)),
            pl.BlockSpec((CTX_LEN, NA_HEAD_DIM), lambda h, b: (b, 2 * nh + h)),
            pl.BlockSpec((4, None, gq, gk), lambda h, b: (0, l * nh + h, 0, 0)),
        ],
        out_specs=pl.BlockSpec((SEQ, NA_HEAD_DIM), lambda h, b: (b, h)),
        out_shape=jax.ShapeDtypeStruct((BATCH * SEQ, D_NA), BF16),
        scratch_shapes=[pltpu.VMEM((SEQ, 2 * V_HEAD), BF16), pltpu.VMEM((CTX_LEN, 2 * V_HEAD), BF16)],
        compiler_params=_params(("parallel", "parallel")),
        name="na_latent",
    )(p2_lat, p2_lat, p2_lat, p2_ctx, p2_ctx, tables)


def _merge_kernel(a_ref, b_ref, h_ref, mod_ref, ga_ref, gb_ref, wa_ref, wb_ref, o_ref, *, k0):
    an = _rms(a_ref[...].astype(F32), ga_ref[...]).astype(BF16)
    bn = _rms(b_ref[...].astype(F32), gb_ref[...]).astype(BF16)
    y = _dot(an, wa_ref[...]) + _dot(bn, wb_ref[...])
    o_ref[...] = h_ref[...] + mod_ref[0, k0:k0 + 1, :] * y


def _merge(a, b, h, mod, row_of_tile, out_norm, w_out, k0):
    t = h.shape[0]
    g2 = out_norm.reshape(2, 1, D_MLA_OUT)
    return pl.pallas_call(
        functools.partial(_merge_kernel, k0=k0),
        grid=(t // TM,),
        in_specs=[
            pl.BlockSpec((TM, D_MLA_OUT), lambda i: (i, 0)),
            pl.BlockSpec((TM, D_NA), lambda i: (i, 0)),
            pl.BlockSpec((TM, D_MODEL), lambda i: (i, 0)),
            _mod_spec(row_of_tile, TM),
            pl.BlockSpec((None, 1, D_MLA_OUT), lambda i: (0, 0, 0)),
            pl.BlockSpec((None, 1, D_NA), lambda i: (1, 0, 0)),
            pl.BlockSpec((D_MLA_OUT, D_MODEL), lambda i: (0, 0)),
            pl.BlockSpec((D_NA, D_MODEL), lambda i: (1, 0)),
        ],
        out_specs=pl.BlockSpec((TM, D_MODEL), lambda i: (i, 0)),
        out_shape=jax.ShapeDtypeStruct((t, D_MODEL), F32),
        compiler_params=_params(("parallel",)),
        name="merge",
    )(a, b, h, mod, g2, g2, w_out, w_out)


def _rope_tables():
    t = np.arange(SEQ)
    pos = np.stack([t // GRID_W, t % GRID_W], axis=-1).astype(np.float32)
    inv_freq = jnp.asarray(ROPE_THETA, F32) ** (-jnp.arange(N_FREQ, dtype=F32) / N_FREQ)
    ang = jnp.asarray(pos)[:, :, None] * inv_freq
    cos, sin = jnp.cos(ang), jnp.sin(ang)
    cos_t = jnp.concatenate([cos, cos], axis=-1).reshape(SEQ, QK_ROPE)
    sin_t = jnp.concatenate([-sin, sin], axis=-1).reshape(SEQ, QK_ROPE)
    pad = jnp.zeros((SEQ, LANES - QK_ROPE), F32)
    cos_t = jnp.concatenate([cos_t, pad], axis=-1)
    sin_t = jnp.concatenate([sin_t, pad], axis=-1)
    ident = jnp.concatenate([jnp.ones((TM, QK_ROPE), F32), jnp.zeros((TM, LANES - QK_ROPE), F32)], axis=-1)
    return (jnp.concatenate([cos_t, ident], axis=0),
            jnp.concatenate([sin_t, jnp.zeros((TM, LANES), F32)], axis=0))


def _rotate_half_cols(w):
    blocks = w.reshape(w.shape[:-1] + (2, 2, N_FREQ))
    return blocks[..., ::-1, :].reshape(w.shape)


def _layer_weights(w_in, w_uq, w_ukv):
    kr = w_in[:, 2 * Q_LORA:2 * Q_LORA + QK_ROPE]
    zpad = jnp.zeros((D_MODEL, LANES - QK_ROPE), F32)
    w_p1 = jnp.concatenate([w_in[:, :2 * Q_LORA], kr, zpad, _rotate_half_cols(kr), zpad], axis=1).astype(BF16)
    w_p2 = w_in[:, 2 * Q_LORA + QK_ROPE:].astype(BF16)
    uq = w_uq.reshape(Q_LORA, MLA_HEADS, QK_NOPE + QK_ROPE)
    hz = jnp.zeros((Q_LORA, MLA_HEADS, QK_PAD - QK_NOPE - QK_ROPE), F32)
    wa = jnp.concatenate([uq, hz], axis=-1).reshape(Q_LORA, MLA_HEADS * QK_PAD).astype(BF16)
    wb = jnp.concatenate([_rotate_half_cols(uq[..., QK_NOPE:]), hz], axis=-1)
    wb = wb.reshape(Q_LORA, MLA_HEADS * LANES).astype(BF16)
    ukv = w_ukv.reshape(KV_LORA, MLA_HEADS, QK_NOPE + V_HEAD)
    wk = ukv[..., :QK_NOPE].reshape(KV_LORA, MLA_HEADS * QK_NOPE).astype(BF16)
    wv = ukv[..., QK_NOPE:].reshape(KV_LORA, D_MLA_OUT).astype(BF16)
    return w_p1, w_p2, wa, wb, wk, wv


def kernel(x, c, ctx, c_ctx, ada_w, ada_b, ffn1_norm, ffn1_w1, ffn1_w3, ffn1_w2, mix_norm, w_in, qa_norm, w_uq,
           kva_norm, w_ukv, na_rpb, out_norm, w_out, ffn2_norm, ffn2_w1, ffn2_w3, ffn2_w2, final_norm):
    c_rows = jnp.concatenate([c, c_ctx[None, :], jnp.zeros((MOD_ROWS - BATCH - 1, D_MODEL), F32)], axis=0)
    mod = _modulation(c_rows, ada_w, ada_b)
    cos_t, sin_t = _rope_tables()
    lat_tab = lambda i, tm: i % (SEQ // tm)
    ctx_tab = lambda i, tm: SEQ // tm
    tables = _na_tables(na_rpb.reshape(DEPTH * NA_HEADS, 2 * WIN_H - 1, 2 * WIN_W - 1))
    ffn1_w = tuple(_cast_bf16(w) for w in (ffn1_w1, ffn1_w3, ffn1_w2))
    ffn2_w = tuple(_cast_bf16(w) for w in (ffn2_w1, ffn2_w3, ffn2_w2))

    h = x.reshape(BATCH * SEQ, D_MODEL)
    hc = ctx.reshape(BATCH * CTX_LEN, D_MODEL)
    for l in range(DEPTH):
        last = l == DEPTH - 1
        lat_row = lambda i, tm, l=l: l * MOD_ROWS + (i * tm) // SEQ
        ctx_row = lambda i, tm, l=l: l * MOD_ROWS + CTX_MOD_ROW
        h = _ffn(h, mod, lat_row, ffn1_norm[l], *ffn1_w, l, 0)
        hc = _ffn(hc, mod, ctx_row, ffn1_norm[l], *ffn1_w, l, 0)

        w_p1, w_p2, wa, wb, wk, wv = _layer_weights(w_in[l], w_uq[l], w_ukv[l])
        mix_w = (mix_norm[l], w_p1, w_p2, qa_norm[l], kva_norm[l], wa, wb, wk, wv, cos_t, sin_t, 3)
        q, k, v, p2 = _mixer_in(h, mod, lat_row, lat_tab, *mix_w)
        qc, kc, vc, p2c = _mixer_in(hc, mod, ctx_row, ctx_tab, *mix_w)
        a_lat = _mla_latent(q, k, kc, v, vc)
        b_lat = _na_latent(p2, p2c, tables, l)
        h = _merge(a_lat, b_lat, h, mod, lat_row, out_norm[l], w_out[l].astype(BF16), 5)
        h = _ffn(h, mod, lat_row, ffn2_norm[l], *ffn2_w, l, 6, final_g=final_norm if last else None)
        if not last:
            a_ctx = _ctx_attention(qc, kc, vc, QK_PAD, 0, 0, 0)
            b_ctx = _ctx_attention(p2c, p2c, p2c, NA_HEAD_DIM, 0, 1, 2)
            hc = _merge(a_ctx, b_ctx, hc, mod, ctx_row, out_norm[l], w_out[l].astype(BF16), 5)
            hc = _ffn(hc, mod, ctx_row, ffn2_norm[l], *ffn2_w, l, 6)
    return h.reshape(BATCH, SEQ, D_MODEL)
```
